```python
import jax, jax.numpy as jnp
from jax import lax
import numpy as np

D_MODEL = 2048
BATCH = 4
SEQ = 8192
DEPTH = 1

GRID_W = 64
CTX_LEN = 256
ROPE_BASE = 10000.0
EPS = 1e-6
Q_BLK = 128

A_HEADS = 16
A_KV_HEADS = 4
A_HEAD_DIM = 64
WINDOW = 128

B_HEADS = 8
B_NOPE_DIM = 128
B_ROPE_DIM = 64
B_QK_DIM = B_NOPE_DIM + B_ROPE_DIM
B_V_DIM = 128
Q_LORA_RANK = 512
KV_LORA_RANK = 256

MIX_WIDTH = A_HEADS * A_HEAD_DIM + B_HEADS * B_V_DIM
IN_SPLITS = (A_HEADS * A_HEAD_DIM, A_KV_HEADS * A_HEAD_DIM, A_KV_HEADS * A_HEAD_DIM,
             Q_LORA_RANK, KV_LORA_RANK, B_ROPE_DIM)
IN_COLS = sum(IN_SPLITS)

N_EXPERTS = 64
TOP_K = 8
N_GROUPS = 8
TOPK_GROUPS = 4
EXPERT_FF = 512
SHARED_FF = 512
ROUTED_SCALE = 2.5
EXPERT_BLK = 128

kernel_name = "hybrid_swa_mla_moe_dit_block"


def rmsnorm(x, g):
    xf = x.astype(jnp.float32)
    y = xf * lax.rsqrt(jnp.mean(xf * xf, axis=-1, keepdims=True) + EPS)
    return (y * g.astype(jnp.float32)).astype(x.dtype)


def modulate(x, shift, scale):
    return x * (1 + scale) + shift


def heads(t, n):
    return t.reshape(*t.shape[:-1], n, t.shape[-1] // n)


def axial_rope_tables(rows, rot_dim):
    r = jnp.repeat(jnp.arange(rows, dtype=jnp.float32), GRID_W)
    cidx = jnp.tile(jnp.arange(GRID_W, dtype=jnp.float32), rows)
    n_freq = rot_dim // 4
    inv = ROPE_BASE ** (-jnp.arange(n_freq, dtype=jnp.float32) / n_freq)
    ang = jnp.concatenate([r[:, None] * inv, cidx[:, None] * inv], axis=-1)
    return jnp.cos(ang), jnp.sin(ang)


def rope_or_id(x, rope):
    if rope is None:
        return x
    cos, sin = rope
    c = cos[None, :, None, :].astype(x.dtype)
    s = sin[None, :, None, :].astype(x.dtype)
    x1, x2 = jnp.split(x, 2, axis=-1)
    return jnp.concatenate([x1 * c - x2 * s, x2 * c + x1 * s], axis=-1)


def softmax_with_sink(scores, sink):
    if sink is None:
        return jax.nn.softmax(scores, axis=-1)
    full = jnp.concatenate([scores, jnp.broadcast_to(sink, scores.shape[:-1] + (1,))], axis=-1)
    return jax.nn.softmax(full, axis=-1)[..., :-1]


def dense_attn(q, k, v, sink):
    b, lq, h, d = q.shape
    hkv = k.shape[2]
    g = h // hkv
    qg = q.reshape(b, lq, hkv, g, d)
    s = jnp.einsum('bqhgd,bkhd->bhgqk', qg, k).astype(jnp.float32) * (d ** -0.5)
    sink_g = None if sink is None else sink.reshape(hkv, g)[None, :, :, None, None].astype(jnp.float32)
    p = softmax_with_sink(s, sink_g).astype(v.dtype)
    return jnp.einsum('bhgqk,bkhd->bqhgd', p, v).reshape(b, lq, h * v.shape[-1])


def windowed_gqa(q, k, v, k_ctx, v_ctx, sink):
    b, s_len, h, d = q.shape
    hkv = k.shape[2]
    g = h // hkv
    nb = s_len // Q_BLK
    span = Q_BLK + 2 * WINDOW
    pad = ((0, 0), (WINDOW, WINDOW), (0, 0), (0, 0))
    kp = jnp.pad(k, pad)
    vp = jnp.pad(v, pad)
    q_blocks = q.reshape(b, nb, Q_BLK, hkv, g, d).transpose(1, 0, 2, 3, 4, 5)
    r = jnp.arange(Q_BLK)
    j = jnp.arange(span)
    in_band = jnp.abs(j[None, :] - WINDOW - r[:, None]) <= WINDOW
    sink_g = sink.reshape(hkv, g)[None, :, :, None, None].astype(jnp.float32)
    scale = d ** -0.5

    def block(args):
        qb, i = args
        start = i * Q_BLK
        kb = lax.dynamic_slice_in_dim(kp, start, span, axis=1)
        vb = lax.dynamic_slice_in_dim(vp, start, span, axis=1)
        kpos = start - WINDOW + j
        mask = in_band & ((kpos >= 0) & (kpos < s_len))[None, :]
        s_win = jnp.einsum('bqhgd,bkhd->bhgqk', qb, kb).astype(jnp.float32) * scale
        s_win = jnp.where(mask, s_win, -jnp.inf)
        s_ctx = jnp.einsum('bqhgd,bkhd->bhgqk', qb, k_ctx).astype(jnp.float32) * scale
        p = softmax_with_sink(jnp.concatenate([s_win, s_ctx], axis=-1), sink_g).astype(v.dtype)
        return (jnp.einsum('bhgqk,bkhd->bqhgd', p[..., :span], vb)
                + jnp.einsum('bhgqk,bkhd->bqhgd', p[..., span:], v_ctx))

    out = lax.map(block, (q_blocks, jnp.arange(nb)))
    return out.transpose(1, 0, 2, 3, 4, 5).reshape(b, s_len, h * d)


def mla_q(cq, g, w_uq, rope):
    b, l, _ = cq.shape
    q = (rmsnorm(cq, g) @ w_uq).reshape(b, l, B_HEADS, B_QK_DIM)
    q_nope, q_rope = jnp.split(q, [B_NOPE_DIM], axis=-1)
    return jnp.concatenate([q_nope, rope_or_id(q_rope, rope)], axis=-1)


def mla_kv(ckv, kr, g, w_ukv, rope):
    b, l, _ = ckv.shape
    kv = (rmsnorm(ckv, g) @ w_ukv).reshape(b, l, B_HEADS, B_NOPE_DIM + B_V_DIM)
    k_nope, v = jnp.split(kv, [B_NOPE_DIM], axis=-1)
    k_rope = rope_or_id(kr[:, :, None, :], rope)
    k = jnp.concatenate([k_nope, jnp.broadcast_to(k_rope, (b, l, B_HEADS, B_ROPE_DIM))], axis=-1)
    return k, v


def mla_attn(q, k, v, k_ctx, v_ctx):
    b, s_len, h, dq = q.shape
    nb = s_len // Q_BLK
    kk = jnp.concatenate([k, k_ctx], axis=1)
    vv = jnp.concatenate([v, v_ctx], axis=1)
    q_blocks = q.reshape(b, nb, Q_BLK, h, dq).transpose(1, 0, 2, 3, 4)
    out = lax.map(lambda qb: dense_attn(qb, kk, vv, None), q_blocks)
    return out.transpose(1, 0, 2, 3).reshape(b, s_len, h * B_V_DIM)


def split_proj(h, w_in):
    offs = np.cumsum(IN_SPLITS)[:-1].tolist()
    return jnp.split(h @ w_in, offs, axis=-1)


def swiglu(x, wg, wu, wd):
    return (jax.nn.silu(x @ wg) * (x @ wu)) @ wd


def route(hf, w_router, bias):
    t = hf.shape[0]
    s = jax.nn.sigmoid(hf.astype(jnp.float32) @ w_router.astype(jnp.float32))
    sel = s + bias.astype(jnp.float32)
    grp = sel.reshape(t, N_GROUPS, N_EXPERTS // N_GROUPS)
    g_score = lax.top_k(grp, 2)[0].sum(-1)
    _, g_idx = lax.top_k(g_score, TOPK_GROUPS)
    g_mask = jax.nn.one_hot(g_idx, N_GROUPS, dtype=jnp.float32).sum(1) > 0
    sel = jnp.where(jnp.repeat(g_mask, N_EXPERTS // N_GROUPS, axis=1), sel, -jnp.inf)
    _, idx = lax.top_k(sel, TOP_K)
    w = jnp.take_along_axis(s, idx, axis=1)
    w = w / jnp.sum(w, axis=-1, keepdims=True) * ROUTED_SCALE
    return idx, w.astype(hf.dtype)


def routed_experts(hf, idx, wts, w_gate, w_up, w_down):
    t, d = hf.shape
    a = t * TOP_K
    e_flat = idx.reshape(a)
    order = jnp.argsort(e_flat)
    e_sorted = e_flat[order]
    tok_sorted = (order // TOP_K).astype(jnp.int32)
    w_sorted = wts.reshape(a)[order]
    counts = jnp.bincount(e_flat, length=N_EXPERTS)
    padded = (counts + EXPERT_BLK - 1) // EXPERT_BLK * EXPERT_BLK
    pad_end = jnp.cumsum(padded)
    pad_start = pad_end - padded
    start = jnp.cumsum(counts) - counts
    dest = pad_start[e_sorted] + jnp.arange(a) - start[e_sorted]
    n_blk = -(-a // EXPERT_BLK) + N_EXPERTS
    p_len = n_blk * EXPERT_BLK
    tok_buf = jnp.full((p_len,), t, jnp.int32).at[dest].set(tok_sorted)
    w_buf = jnp.zeros((p_len,), hf.dtype).at[dest].set(w_sorted)
    blk_expert = jnp.minimum(
        jnp.searchsorted(pad_end, jnp.arange(n_blk) * EXPERT_BLK, side='right'), N_EXPERTS - 1)
    h_pad = jnp.concatenate([hf, jnp.zeros((1, d), hf.dtype)], axis=0)

    def step(acc, blk):
        tok, w, e = blk
        yb = swiglu(h_pad[tok], w_gate[e], w_up[e], w_down[e])
        return acc.at[tok].add(yb * w[:, None]), None

    acc, _ = lax.scan(step, jnp.zeros((t + 1, d), hf.dtype),
                      (tok_buf.reshape(n_blk, EXPERT_BLK), w_buf.reshape(n_blk, EXPERT_BLK), blk_expert))
    return acc[:t]


def moe_ffn(h, w_router, router_bias, w_gate, w_up, w_down, ws_gate, ws_up, ws_down):
    b, l, d = h.shape
    hf = h.reshape(b * l, d)
    idx, wts = route(hf, w_router, router_bias)
    y = routed_experts(hf, idx, wts, w_gate, w_up, w_down) + swiglu(hf, ws_gate, ws_up, ws_down)
    return y.reshape(b, l, d)


def setup_inputs(seed: int = 0) -> dict:
    key = jax.random.key(seed)
    ks = jax.random.split(key, 32)
    D = D_MODEL

    def nrm(k, shape, scale):
        return jax.random.normal(k, shape, jnp.float32) * scale

    return {
        "x": nrm(ks[0], (BATCH, SEQ, D), 1.0),
        "c": nrm(ks[1], (BATCH, D), 1.0),
        "ctx": nrm(ks[2], (BATCH, CTX_LEN, D), 1.0),
        "c_ctx": nrm(ks[3], (D,), 1.0),
        "w_mod": nrm(ks[4], (DEPTH, D, 6 * D), 0.5 * D ** -0.5),
        "b_mod": nrm(ks[5], (DEPTH, 6 * D), 0.02),
        "norm_attn_g": 1.0 + nrm(ks[6], (DEPTH, D), 0.01),
        "norm_ffn_g": 1.0 + nrm(ks[7], (DEPTH, D), 0.01),
        "w_in": nrm(ks[8], (DEPTH, D, IN_COLS), D ** -0.5),
        "attn_sink": nrm(ks[9], (DEPTH, A_HEADS), 0.5),
        "q_a_norm_g": 1.0 + nrm(ks[10], (DEPTH, Q_LORA_RANK), 0.01),
        "w_uq": nrm(ks[11], (DEPTH, Q_LORA_RANK, B_HEADS * B_QK_DIM), Q_LORA_RANK ** -0.5),
        "kv_a_norm_g": 1.0 + nrm(ks[12], (DEPTH, KV_LORA_RANK), 0.01),
        "w_ukv": nrm(ks[13], (DEPTH, KV_LORA_RANK, B_HEADS * (B_NOPE_DIM + B_V_DIM)), KV_LORA_RANK ** -0.5),
        "w_out": nrm(ks[14], (DEPTH, MIX_WIDTH, D), MIX_WIDTH ** -0.5),
        "w_router": nrm(ks[15], (DEPTH, D, N_EXPERTS), D ** -0.5),
        "router_bias": nrm(ks[16], (DEPTH, N_EXPERTS), 0.01),
        "w_gate": nrm(ks[17], (DEPTH, N_EXPERTS, D, EXPERT_FF), D ** -0.5),
        "w_up": nrm(ks[18], (DEPTH, N_EXPERTS, D, EXPERT_FF), D ** -0.5),
        "w_down": nrm(ks[19], (DEPTH, N_EXPERTS, EXPERT_FF, D), EXPERT_FF ** -0.5),
        "ws_gate": nrm(ks[20], (DEPTH, D, SHARED_FF), D ** -0.5),
        "ws_up": nrm(ks[21], (DEPTH, D, SHARED_FF), D ** -0.5),
        "ws_down": nrm(ks[22], (DEPTH, SHARED_FF, D), SHARED_FF ** -0.5),
        "norm_final_g": 1.0 + nrm(ks[23], (D,), 0.01),
    }


def reference(x, c, ctx, c_ctx, w_mod, b_mod, norm_attn_g, norm_ffn_g, w_in, attn_sink,
              q_a_norm_g, w_uq, kv_a_norm_g, w_ukv, w_out, w_router, router_bias,
              w_gate, w_up, w_down, ws_gate, ws_up, ws_down, norm_final_g):
    b, s_len, d = x.shape
    ROWS = s_len // GRID_W
    rope_a = axial_rope_tables(ROWS, A_HEAD_DIM)
    rope_b = axial_rope_tables(ROWS, B_ROPE_DIM)

    for l in range(DEPTH):
        last = l == DEPTH - 1
        mod = (jax.nn.silu(c) @ w_mod[l] + b_mod[l])[:, None, :]
        mod_c = jax.nn.silu(c_ctx) @ w_mod[l] + b_mod[l]
        sh_a, sc_a, g_a, sh_f, sc_f, g_f = jnp.split(mod, 6, axis=-1)
        sh_ac, sc_ac, g_ac, sh_fc, sc_fc, g_fc = jnp.split(mod_c, 6, axis=-1)

        h = modulate(rmsnorm(x, norm_attn_g[l]), sh_a, sc_a)
        hc = modulate(rmsnorm(ctx, norm_attn_g[l]), sh_ac, sc_ac)
        qa, ka, va, cq, ckv, kr = split_proj(h, w_in[l])
        qa_c, ka_c, va_c, cq_c, ckv_c, kr_c = split_proj(hc, w_in[l])

        qa = rope_or_id(heads(qa, A_HEADS), rope_a)
        ka = rope_or_id(heads(ka, A_KV_HEADS), rope_a)
        va = heads(va, A_KV_HEADS)
        ka_c = heads(ka_c, A_KV_HEADS)
        va_c = heads(va_c, A_KV_HEADS)
        o_a = windowed_gqa(qa, ka, va, ka_c, va_c, attn_sink[l])

        qb = mla_q(cq, q_a_norm_g[l], w_uq[l], rope_b)
        kb, vb = mla_kv(ckv, kr, kv_a_norm_g[l], w_ukv[l], rope_b)
        kb_c, vb_c = mla_kv(ckv_c, kr_c, kv_a_norm_g[l], w_ukv[l], None)
        o_b = mla_attn(qb, kb, vb, kb_c, vb_c)

        x_new = x + g_a * (jnp.concatenate([o_a, o_b], axis=-1) @ w_out[l])

        if not last:
            qa_c = heads(qa_c, A_HEADS)
            qb_c = mla_q(cq_c, q_a_norm_g[l], w_uq[l], None)
            o_c = jnp.concatenate([dense_attn(qa_c, ka_c, va_c, attn_sink[l]),
                                   dense_attn(qb_c, kb_c, vb_c, None)], axis=-1)
            ctx = ctx + g_ac * (o_c @ w_out[l])
            hc2 = modulate(rmsnorm(ctx, norm_ffn_g[l]), sh_fc, sc_fc)
            ctx = ctx + g_fc * moe_ffn(hc2, w_router[l], router_bias[l], w_gate[l], w_up[l], w_down[l],
                                       ws_gate[l], ws_up[l], ws_down[l])

        x = x_new
        h2 = modulate(rmsnorm(x, norm_ffn_g[l]), sh_f, sc_f)
        x = x + g_f * moe_ffn(h2, w_router[l], router_bias[l], w_gate[l], w_up[l], w_down[l],
                              ws_gate[l], ws_up[l], ws_down[l])

    return rmsnorm(x, norm_final_g)
```

```python
import functools
import math

import numpy as np
import jax
import jax.numpy as jnp
from jax import lax
from jax.experimental import pallas as pl
from jax.experimental.pallas import tpu as pltpu

F32 = jnp.float32
BF16 = jnp.bfloat16

GRID_W = 64
ROPE_BASE = 10000.0
EPS = 1e-6
A_HEADS = 16
A_KV_HEADS = 4
A_HEAD_DIM = 64
WINDOW = 128
B_HEADS = 8
B_NOPE_DIM = 128
B_ROPE_DIM = 64
B_QK_DIM = B_NOPE_DIM + B_ROPE_DIM
B_V_DIM = 128
Q_LORA_RANK = 512
KV_LORA_RANK = 256
N_EXPERTS = 64
TOP_K = 8
N_GROUPS = 8
TOPK_GROUPS = 4
ROUTED_SCALE = 2.5

LOG2E = math.log2(math.e)
NEG_BIG = -1e30

VMEM_LIMIT_BYTES = 56 * 1024 * 1024

ROW_TILE = 256
SWA_BLOCK = 128
MLA_TQ = 1024
MLA_TK = 768
EXPERT_TILE = 256
HEAD_PAD = 256

IN_COLS_PADDED = 2432


def _cparams(sem):
    return pltpu.CompilerParams(dimension_semantics=sem, vmem_limit_bytes=VMEM_LIMIT_BYTES)


def _const_spec(shape):
    n = len(shape)
    return pl.BlockSpec(shape, lambda *_: (0,) * n, pipeline_mode=pl.Buffered(1))


def _mod_kernel(c_ref, w_ref, b_ref, o_ref):
    c = c_ref[...]
    a = c * jax.nn.sigmoid(c)
    o_ref[...] = jnp.dot(a, w_ref[...], preferred_element_type=F32,
                         precision=lax.Precision.HIGHEST) + b_ref[...]


def _mod(c_rows, w_mod, b_mod):
    d, n = w_mod.shape
    tn = 1024
    return pl.pallas_call(
        _mod_kernel,
        grid=(n // tn,),
        in_specs=[pl.BlockSpec((8, d), lambda j: (0, 0)),
                  pl.BlockSpec((d, tn), lambda j: (0, j)),
                  pl.BlockSpec((1, tn), lambda j: (0, j))],
        out_specs=pl.BlockSpec((8, tn), lambda j: (0, j)),
        out_shape=jax.ShapeDtypeStruct((8, n), F32),
        compiler_params=_cparams(("arbitrary",)),
        name="mod",
    )(c_rows, w_mod, b_mod)


def _rms(x, g):
    var = jnp.mean(x * x, axis=-1, keepdims=True)
    return x * lax.rsqrt(var + EPS) * g


def _inproj_kernel(x_ref, ms_ref, g_ref, w_in_ref, gq_ref, w_uqt_ref, gkv_ref, w_uk_ref, w_uvt_ref,
                   rope_a_ref, rope_k_ref, rope_t_ref, *rest, sa, sb):
    qa_ref, ka_ref, va_ref, qbt_ref, kb_ref, vbt_ref = rest[-6:]
    x = x_ref[0]
    shift = ms_ref[0, 0:1, :]
    scale = ms_ref[0, 1:2, :]
    h = _rms(x, g_ref[...]) * (1.0 + scale) + shift
    z = jnp.dot(h.astype(BF16), w_in_ref[...], preferred_element_type=F32)

    cos_a = rope_a_ref[:, 0:128]
    sin_a = rope_a_ref[:, 128:256]
    for r in range(4):
        q1 = z[:, 256 * r:256 * r + 128]
        q2 = z[:, 256 * r + 128:256 * r + 256]
        qa_ref[0, :, 256 * r:256 * r + 128] = ((q1 * cos_a - q2 * sin_a) * sa).astype(BF16)
        qa_ref[0, :, 256 * r + 128:256 * r + 256] = ((q2 * cos_a + q1 * sin_a) * sa).astype(BF16)
    k1 = z[:, 1024:1152]
    k2 = z[:, 1152:1280]
    ka_ref[0, :, 0:128] = (k1 * cos_a - k2 * sin_a).astype(BF16)
    ka_ref[0, :, 128:256] = (k2 * cos_a + k1 * sin_a).astype(BF16)
    va_ref[0] = z[:, 1280:1536].astype(BF16)

    cqn = _rms(z[:, 1536:2048], gq_ref[...])
    qt = jnp.dot(w_uqt_ref[...], cqn.T.astype(BF16), preferred_element_type=F32)
    cos_t = rope_t_ref[0:32, :]
    sin_t = rope_t_ref[32:64, :]
    for hh in range(B_HEADS):
        base = HEAD_PAD * hh
        x1 = qt[base + 128:base + 160]
        x2 = qt[base + 160:base + 192]
        qbt_ref[0, hh, 0:128, :] = (qt[base:base + 128] * sb).astype(BF16)
        qbt_ref[0, hh, 128:160, :] = ((x1 * cos_t - x2 * sin_t) * sb).astype(BF16)
        qbt_ref[0, hh, 160:192, :] = ((x2 * cos_t + x1 * sin_t) * sb).astype(BF16)
        qbt_ref[0, hh, 192:256, :] = jnp.zeros((64, x.shape[0]), BF16)

    ckvn = _rms(z[:, 2048:2304], gkv_ref[...])
    knope = jnp.dot(ckvn.astype(BF16), w_uk_ref[...], preferred_element_type=F32)
    vt = jnp.dot(w_uvt_ref[...], ckvn.T.astype(BF16), preferred_element_type=F32)
    vbt_ref[0] = vt.astype(BF16)
    t = z[:, 2304:2432] * rope_k_ref[...]
    lane = lax.broadcasted_iota(jnp.int32, t.shape, 1)
    krp = jnp.where(lane < 64, t + pltpu.roll(t, 64, 1), 0.0).astype(BF16)
    for hh in range(B_HEADS):
        kb_ref[0, :, HEAD_PAD * hh:HEAD_PAD * hh + 128] = knope[:, 128 * hh:128 * hh + 128].astype(BF16)
        kb_ref[0, :, HEAD_PAD * hh + 128:HEAD_PAD * hh + 256] = krp


def _inproj(x, ms, g_attn, w_in_p, gq, w_uqt, gkv, w_uk, w_uvt, rope_a, rope_k, rope_t,
            l_total, row_block0, prev=None):
    b, n, d = x.shape
    tm = ROW_TILE
    nt = n // tm
    sa = (A_HEAD_DIM ** -0.5) * LOG2E
    sb = (B_QK_DIM ** -0.5) * LOG2E
    out_shape = [
        jax.ShapeDtypeStruct((b, l_total, A_HEADS * A_HEAD_DIM), BF16),
        jax.ShapeDtypeStruct((b, l_total, A_KV_HEADS * A_HEAD_DIM), BF16),
        jax.ShapeDtypeStruct((b, l_total, A_KV_HEADS * A_HEAD_DIM), BF16),
        jax.ShapeDtypeStruct((b, B_HEADS, HEAD_PAD, l_total), BF16),
        jax.ShapeDtypeStruct((b, l_total, B_HEADS * HEAD_PAD), BF16),
        jax.ShapeDtypeStruct((b, B_HEADS * B_V_DIM, l_total), BF16),
    ]
    r0 = row_block0
    out_specs = [
        pl.BlockSpec((1, tm, 1024), lambda bi, i: (bi, r0 + i, 0)),
        pl.BlockSpec((1, tm, 256), lambda bi, i: (bi, r0 + i, 0)),
        pl.BlockSpec((1, tm, 256), lambda bi, i: (bi, r0 + i, 0)),
        pl.BlockSpec((1, B_HEADS, HEAD_PAD, tm), lambda bi, i: (bi, 0, 0, r0 + i)),
        pl.BlockSpec((1, tm, B_HEADS * HEAD_PAD), lambda bi, i: (bi, r0 + i, 0)),
        pl.BlockSpec((1, B_HEADS * B_V_DIM, tm), lambda bi, i: (bi, 0, r0 + i)),
    ]
    in_specs = [
        pl.BlockSpec((1, tm, d), lambda bi, i: (bi, i, 0)),
        pl.BlockSpec((1, 2, d), lambda bi, i: (bi, 0, 0)),
        _const_spec((1, d)),
        _const_spec(w_in_p.shape),
        _const_spec((1, Q_LORA_RANK)),
        _const_spec(w_uqt.shape),
        _const_spec((1, KV_LORA_RANK)),
        _const_spec(w_uk.shape),
        _const_spec(w_uvt.shape),
        pl.BlockSpec((tm, 256), lambda bi, i: (r0 + i, 0)),
        pl.BlockSpec((tm, 128), lambda bi, i: (r0 + i, 0)),
        pl.BlockSpec((64, tm), lambda bi, i: (0, r0 + i)),
    ]
    args = [x, ms, g_attn, w_in_p, gq, w_uqt, gkv, w_uk, w_uvt, rope_a, rope_k, rope_t]
    aliases = {}
    if prev is not None:
        for k, p in enumerate(prev):
            in_specs.append(pl.BlockSpec(memory_space=pl.ANY))
            aliases[len(args)] = k
            args.append(p)
    return pl.pallas_call(
        functools.partial(_inproj_kernel, sa=sa, sb=sb),
        grid=(b, nt),
        in_specs=in_specs,
        out_specs=out_specs,
        out_shape=out_shape,
        input_output_aliases=aliases,
        compiler_params=_cparams(("arbitrary", "arbitrary")),
        name="inproj_ctx" if prev is not None else "inproj",
    )(*args)


def _swa_kernel(q_ref, kp_ref, kc_ref, kn_ref, vp_ref, vc_ref, vn_ref, kx_ref, vx_ref, sink_ref, o_ref, *, nblk):
    i = pl.program_id(1)
    q = q_ref[0]
    lane256 = lax.broadcasted_iota(jnp.int32, (SWA_BLOCK, 256), 1)
    zero = jnp.zeros((SWA_BLOCK, 256), BF16)
    parts = []
    for r in range(4):
        qr = q[:, 256 * r:256 * r + 256]
        for j in range(A_KV_HEADS):
            sel = (lane256 % 128) // 32 == j
            parts.append(jnp.where(sel, qr, zero))
    lhs = jnp.concatenate(parts, axis=0)
    kwin = jnp.concatenate([kp_ref[0], kc_ref[0], kn_ref[0]], axis=0)
    nt = (((1,), (1,)), ((), ()))
    s_win = lax.dot_general(lhs, kwin, nt, preferred_element_type=F32)
    s_ctx = lax.dot_general(lhs, kx_ref[0], nt, preferred_element_type=F32)

    rq = lax.broadcasted_iota(jnp.int32, (SWA_BLOCK, 3 * SWA_BLOCK), 0)
    ck = lax.broadcasted_iota(jnp.int32, (SWA_BLOCK, 3 * SWA_BLOCK), 1)
    band = jnp.abs(ck - WINDOW - rq) <= WINDOW
    band = band & ((ck >= SWA_BLOCK) | (i > 0)) & ((ck < 2 * SWA_BLOCK) | (i < nblk - 1))
    s3 = s_win.reshape(16, SWA_BLOCK, 3 * SWA_BLOCK)
    s3 = jnp.where(band[None], s3, NEG_BIG)
    s_win = s3.reshape(16 * SWA_BLOCK, 3 * SWA_BLOCK)

    sink = sink_ref[...]
    m = jnp.maximum(jnp.maximum(jnp.max(s_win, axis=-1, keepdims=True),
                                jnp.max(s_ctx, axis=-1, keepdims=True)), sink)
    p_win = jnp.exp2(s_win - m)
    p_ctx = jnp.exp2(s_ctx - m)
    den = (jnp.sum(p_win, axis=-1, keepdims=True) + jnp.sum(p_ctx, axis=-1, keepdims=True)
           + jnp.exp2(sink - m))
    vwin = jnp.concatenate([vp_ref[0], vc_ref[0], vn_ref[0]], axis=0)
    res = (jnp.dot(p_win.astype(BF16), vwin, preferred_element_type=F32)
           + jnp.dot(p_ctx.astype(BF16), vx_ref[0], preferred_element_type=F32))
    res = res * (1.0 / den)
    lane64 = lane256 // 64
    for r in range(4):
        acc = jnp.zeros((SWA_BLOCK, 256), F32)
        for j in range(A_KV_HEADS):
            s = 4 * r + j
            acc = jnp.where(lane64 == j, res[SWA_BLOCK * s:SWA_BLOCK * (s + 1)], acc)
        o_ref[0, :, 256 * r:256 * r + 256] = acc.astype(BF16)


def _swa(qa, ka, va, sink_rows, s_len, c_len):
    b = qa.shape[0]
    nblk = s_len // SWA_BLOCK
    cblk = s_len // c_len
    kv = lambda f: pl.BlockSpec((1, SWA_BLOCK, 256), f)
    prev_i = lambda bi, i: (bi, jnp.maximum(i - 1, 0), 0)
    cur_i = lambda bi, i: (bi, i, 0)
    next_i = lambda bi, i: (bi, jnp.minimum(i + 1, nblk - 1), 0)
    ctx_spec = pl.BlockSpec((1, c_len, 256), lambda bi, i: (bi, cblk, 0))
    return pl.pallas_call(
        functools.partial(_swa_kernel, nblk=nblk),
        grid=(b, nblk),
        in_specs=[pl.BlockSpec((1, SWA_BLOCK, 1024), cur_i),
                  kv(prev_i), kv(cur_i), kv(next_i), kv(prev_i), kv(cur_i), kv(next_i),
                  ctx_spec, ctx_spec, _const_spec(sink_rows.shape)],
        out_specs=pl.BlockSpec((1, SWA_BLOCK, 1024), cur_i),
        out_shape=jax.ShapeDtypeStruct((b, s_len, 1024), BF16),
        compiler_params=_cparams(("arbitrary", "arbitrary")),
        name="swa",
    )(qa, ka, ka, ka, va, va, va, ka, va, sink_rows)


def _mla_kernel(qt_ref, k_ref, vt_ref, o_ref, m_ref, l_ref, acc_ref, *, nk):
    kj = pl.program_id(2)

    @pl.when(kj == 0)
    def _():
        m_ref[...] = jnp.full(m_ref.shape, NEG_BIG, F32)
        l_ref[...] = jnp.zeros(l_ref.shape, F32)
        acc_ref[...] = jnp.zeros(acc_ref.shape, F32)

    for hh in range(B_HEADS):
        qt = qt_ref[0, hh]
        kh = k_ref[0, :, HEAD_PAD * hh:HEAD_PAD * (hh + 1)]
        st = jnp.dot(kh, qt, preferred_element_type=F32)
        m_old = m_ref[hh]
        m_new = jnp.maximum(m_old, jnp.max(st, axis=0, keepdims=True))
        alpha = jnp.exp2(m_old - m_new)
        p = jnp.exp2(st - m_new)
        l_ref[hh] = alpha * l_ref[hh] + jnp.sum(p, axis=0, keepdims=True)
        m_ref[hh] = m_new
        vth = vt_ref[0, B_V_DIM * hh:B_V_DIM * (hh + 1), :]
        pv = jnp.dot(vth, p.astype(BF16), preferred_element_type=F32)
        acc_ref[B_V_DIM * hh:B_V_DIM * (hh + 1), :] = alpha * acc_ref[B_V_DIM * hh:B_V_DIM * (hh + 1), :] + pv

    @pl.when(kj == nk - 1)
    def _():
        for hh in range(B_HEADS):
            o = acc_ref[B_V_DIM * hh:B_V_DIM * (hh + 1), :] * (1.0 / l_ref[hh])
            o_ref[0, :, B_V_DIM * hh:B_V_DIM * (hh + 1)] = o.T.astype(BF16)


def _mla(qbt, kb, vbt, s_len):
    b, _, _, l_total = qbt.shape
    tq = min(MLA_TQ, s_len)
    tk = MLA_TK if l_total % MLA_TK == 0 else 256
    nq = s_len // tq
    nk = l_total // tk
    return pl.pallas_call(
        functools.partial(_mla_kernel, nk=nk),
        grid=(b, nq, nk),
        in_specs=[pl.BlockSpec((1, B_HEADS, HEAD_PAD, tq), lambda bi, qi, kj: (bi, 0, 0, qi)),
                  pl.BlockSpec((1, tk, B_HEADS * HEAD_PAD), lambda bi, qi, kj: (bi, kj, 0)),
                  pl.BlockSpec((1, B_HEADS * B_V_DIM, tk), lambda bi, qi, kj: (bi, 0, kj))],
        out_specs=pl.BlockSpec((1, tq, B_HEADS * B_V_DIM), lambda bi, qi, kj: (bi, qi, 0)),
        out_shape=jax.ShapeDtypeStruct((b, s_len, B_HEADS * B_V_DIM), BF16),
        scratch_shapes=[pltpu.VMEM((B_HEADS, 1, tq), F32), pltpu.VMEM((B_HEADS, 1, tq), F32),
                        pltpu.VMEM((B_HEADS * B_V_DIM, tq), F32)],
        compiler_params=_cparams(("arbitrary", "arbitrary", "arbitrary")),
        name="mla",
    )(qbt, kb, vbt)


def _post_kernel(oa_ref, ob_ref, x_ref, mod_ref, g_ref, woa_ref, wob_ref, wr_ref, wsg_ref, wsu_ref, wsd_ref,
                 h2_ref, lg_ref, x2_ref):
    att = (jnp.dot(oa_ref[0], woa_ref[...], preferred_element_type=F32)
           + jnp.dot(ob_ref[0], wob_ref[...], preferred_element_type=F32))
    g_a = mod_ref[0, 0:1, :]
    shift = mod_ref[0, 1:2, :]
    scale = mod_ref[0, 2:3, :]
    g_f = mod_ref[0, 3:4, :]
    xn = x_ref[0] + g_a * att
    h2 = _rms(xn, g_ref[...]) * (1.0 + scale) + shift
    h2b = h2.astype(BF16)
    h2_ref[0] = h2b
    lg_ref[0] = jnp.dot(h2, wr_ref[...], preferred_element_type=F32, precision=lax.Precision.HIGHEST)
    gate = jnp.dot(h2b, wsg_ref[...], preferred_element_type=F32)
    up = jnp.dot(h2b, wsu_ref[...], preferred_element_type=F32)
    act = (gate * jax.nn.sigmoid(gate) * up).astype(BF16)
    shared = jnp.dot(act, wsd_ref[...], preferred_element_type=F32)
    x2_ref[0] = xn + g_f * shared


def _post(oa, ob, x, mod4, g_ffn, woa, wob, wr, wsg, wsu, wsd):
    b, s_len, d = x.shape
    tm = ROW_TILE
    row = lambda w: pl.BlockSpec((1, tm, w), lambda bi, i: (bi, i, 0))
    return pl.pallas_call(
        _post_kernel,
        grid=(b, s_len // tm),
        in_specs=[row(1024), row(1024), row(d),
                  pl.BlockSpec((1, 4, d), lambda bi, i: (bi, 0, 0)),
                  _const_spec((1, d)), _const_spec(woa.shape), _const_spec(wob.shape), _const_spec(wr.shape),
                  _const_spec(wsg.shape), _const_spec(wsu.shape), _const_spec(wsd.shape)],
        out_specs=[row(d), row(128), row(d)],
        out_shape=[jax.ShapeDtypeStruct((b, s_len, d), BF16),
                   jax.ShapeDtypeStruct((b, s_len, 128), F32),
                   jax.ShapeDtypeStruct((b, s_len, d), F32)],
        compiler_params=_cparams(("arbitrary", "arbitrary")),
        name="post",
    )(oa, ob, x, mod4, g_ffn, woa, wob, wr, wsg, wsu, wsd)


def _experts_kernel(te_ref, nu_ref, x_ref, wg_ref, wu_ref, wd_ref, y_ref):
    t = pl.program_id(0)

    @pl.when(t < nu_ref[0])
    def _():
        xb = x_ref[...]
        gate = jnp.dot(xb, wg_ref[0], preferred_element_type=F32)
        up = jnp.dot(xb, wu_ref[0], preferred_element_type=F32)
        act = (gate * jax.nn.sigmoid(gate) * up).astype(BF16)
        y_ref[...] = jnp.dot(act, wd_ref[0], preferred_element_type=F32).astype(BF16)


def _experts(tile_expert, n_used, xs, wg, wu, wd):
    p_len, d = xs.shape
    ff = wg.shape[-1]
    tm = EXPERT_TILE
    n_tiles = p_len // tm
    grid_spec = pltpu.PrefetchScalarGridSpec(
        num_scalar_prefetch=2,
        grid=(n_tiles,),
        in_specs=[pl.BlockSpec((tm, d), lambda t, te, nu: (jnp.minimum(t, nu[0] - 1), 0)),
                  pl.BlockSpec((1, d, ff), lambda t, te, nu: (te[t], 0, 0)),
                  pl.BlockSpec((1, d, ff), lambda t, te, nu: (te[t], 0, 0)),
                  pl.BlockSpec((1, ff, d), lambda t, te, nu: (te[t], 0, 0))],
        out_specs=pl.BlockSpec((tm, d), lambda t, te, nu: (jnp.minimum(t, nu[0] - 1), 0)),
    )
    return pl.pallas_call(
        _experts_kernel,
        grid_spec=grid_spec,
        out_shape=jax.ShapeDtypeStruct((p_len, d), BF16),
        compiler_params=_cparams(("arbitrary",)),
        name="experts",
    )(tile_expert, n_used, xs, wg, wu, wd)


def _final_kernel(x2_ref, r_ref, gf_ref, g_ref, o_ref):
    x = x2_ref[0] + gf_ref[0] * r_ref[0]
    o_ref[0] = _rms(x, g_ref[...])


def _final(x2, routed, g_f, g_final):
    b, s_len, d = x2.shape
    tm = 512
    row = pl.BlockSpec((1, tm, d), lambda bi, i: (bi, i, 0))
    return pl.pallas_call(
        _final_kernel,
        grid=(b, s_len // tm),
        in_specs=[row, row, pl.BlockSpec((1, 1, d), lambda bi, i: (bi, 0, 0)), _const_spec((1, d))],
        out_specs=row,
        out_shape=jax.ShapeDtypeStruct((b, s_len, d), F32),
        compiler_params=_cparams(("arbitrary", "arbitrary")),
        name="final",
    )(x2, routed, g_f, g_final)


def _rope_tables(s_len, c_len):
    rows = s_len // GRID_W
    r = jnp.repeat(jnp.arange(rows, dtype=F32), GRID_W)
    cidx = jnp.tile(jnp.arange(GRID_W, dtype=F32), rows)
    n_freq = 16
    inv = ROPE_BASE ** (-jnp.arange(n_freq, dtype=F32) / n_freq)
    ang = jnp.concatenate([r[:, None] * inv, cidx[:, None] * inv], axis=-1)
    cos = jnp.concatenate([jnp.cos(ang), jnp.ones((c_len, 32), F32)], axis=0)
    sin = jnp.concatenate([jnp.sin(ang), jnp.zeros((c_len, 32), F32)], axis=0)
    rope_a = jnp.concatenate([jnp.tile(cos, (1, 4)), jnp.tile(sin, (1, 4))], axis=1)
    rope_k = jnp.concatenate([cos, cos, sin, sin], axis=1)
    rope_t = jnp.concatenate([cos.T, sin.T], axis=0)
    return rope_a, rope_k, rope_t


def _in_weight(w_in):
    qa_cols = np.empty((4, 2, 4, 32), np.int32)
    for r in range(4):
        for p in range(2):
            for j in range(4):
                qa_cols[r, p, j] = (4 * j + r) * 64 + p * 32 + np.arange(32)
    ka_cols = np.empty((2, 4, 32), np.int32)
    for p in range(2):
        for j in range(4):
            ka_cols[p, j] = 1024 + j * 64 + p * 32 + np.arange(32)
    cols = np.concatenate([qa_cols.reshape(-1), ka_cols.reshape(-1), np.arange(1280, 2368)])
    w = w_in[:, cols]
    kr = w_in[:, 2304:2368]
    kr_rot = jnp.concatenate([-kr[:, 32:], kr[:, :32]], axis=1)
    return jnp.concatenate([w, kr_rot], axis=1).astype(BF16)


def _route(logits, bias):
    t = logits.shape[0]
    s = jax.nn.sigmoid(logits)
    sel = s + bias.astype(F32)
    grp = sel.reshape(t, N_GROUPS, N_EXPERTS // N_GROUPS)
    g_score = lax.top_k(grp, 2)[0].sum(-1)
    _, g_idx = lax.top_k(g_score, TOPK_GROUPS)
    g_mask = jax.nn.one_hot(g_idx, N_GROUPS, dtype=F32).sum(1) > 0
    sel = jnp.where(jnp.repeat(g_mask, N_EXPERTS // N_GROUPS, axis=1), sel, -jnp.inf)
    _, idx = lax.top_k(sel, TOP_K)
    w = jnp.take_along_axis(s, idx, axis=1)
    w = w / jnp.sum(w, axis=-1, keepdims=True) * ROUTED_SCALE
    return idx, w


def kernel(x, c, ctx, c_ctx, w_mod, b_mod, norm_attn_g, norm_ffn_g, w_in, attn_sink, q_a_norm_g, w_uq,
           kv_a_norm_g, w_ukv, w_out, w_router, router_bias, w_gate, w_up, w_down, ws_gate, ws_up, ws_down,
           norm_final_g):
    b, s_len, d = x.shape
    c_len = ctx.shape[1]
    l_total = s_len + c_len
    t_tok = b * s_len

    c_rows = jnp.zeros((8, d), F32).at[:b].set(c).at[b].set(c_ctx)
    mod = _mod(c_rows, w_mod[0], b_mod[0][None, :])
    mod6 = mod.reshape(8, 6, d)
    sh_a, sc_a, g_a, sh_f, sc_f, g_f = (mod6[:b, k] for k in range(6))
    ms_x = jnp.stack([sh_a, sc_a], axis=1)
    ms_c = jnp.broadcast_to(jnp.stack([mod6[b, 0], mod6[b, 1]], axis=0)[None], (b, 2, d))

    w_in_p = _in_weight(w_in[0])
    w_uq_h = w_uq[0].reshape(Q_LORA_RANK, B_HEADS, B_QK_DIM)
    w_uqt = jnp.pad(jnp.transpose(w_uq_h, (1, 2, 0)), ((0, 0), (0, HEAD_PAD - B_QK_DIM), (0, 0)))
    w_uqt = w_uqt.reshape(B_HEADS * HEAD_PAD, Q_LORA_RANK).astype(BF16)
    w_ukv_h = w_ukv[0].reshape(KV_LORA_RANK, B_HEADS, B_NOPE_DIM + B_V_DIM)
    w_uk = w_ukv_h[:, :, :B_NOPE_DIM].reshape(KV_LORA_RANK, B_HEADS * B_NOPE_DIM).astype(BF16)
    w_uvt = w_ukv_h[:, :, B_NOPE_DIM:].reshape(KV_LORA_RANK, B_HEADS * B_V_DIM).T.astype(BF16)
    rope_a, rope_k, rope_t = _rope_tables(s_len, c_len)

    g_attn = norm_attn_g[0][None, :]
    gq = q_a_norm_g[0][None, :]
    gkv = kv_a_norm_g[0][None, :]
    proj_w = (g_attn, w_in_p, gq, w_uqt, gkv, w_uk, w_uvt, rope_a, rope_k, rope_t)
    outs = _inproj(x, ms_x, *proj_w, l_total, 0)
    qa, ka, va, qbt, kb, vbt = _inproj(ctx, ms_c, *proj_w, l_total, s_len // ROW_TILE, prev=outs)

    stack_heads = np.array([4 * (s % 4) + s // 4 for s in range(16)], np.int32)
    sink_rows = jnp.repeat(attn_sink[0][stack_heads] * LOG2E, SWA_BLOCK)[:, None].astype(F32)
    o_a = _swa(qa, ka, va, sink_rows, s_len, c_len)
    o_b = _mla(qbt, kb, vbt, s_len)

    oa_rows = np.empty((4, 4, 64), np.int32)
    for r in range(4):
        for j in range(4):
            oa_rows[r, j] = (4 * j + r) * 64 + np.arange(64)
    woa = w_out[0][oa_rows.reshape(-1)].astype(BF16)
    wob = w_out[0][1024:].astype(BF16)
    wr = jnp.pad(w_router[0], ((0, 0), (0, 128 - N_EXPERTS)))
    mod4 = jnp.stack([g_a, sh_f, sc_f, g_f], axis=1)
    h2, logits, x2 = _post(o_a, o_b, x, mod4, norm_ffn_g[0][None, :], woa, wob, wr,
                           ws_gate[0].astype(BF16), ws_up[0].astype(BF16), ws_down[0].astype(BF16))

    idx, wts = _route(logits.reshape(t_tok, 128)[:, :N_EXPERTS], router_bias[0])
    a_len = t_tok * TOP_K
    tm = EXPERT_TILE
    e_flat = idx.reshape(a_len)
    order = jnp.argsort(e_flat)
    e_sorted = e_flat[order]
    counts = jnp.bincount(e_flat, length=N_EXPERTS)
    padded = (counts + tm - 1) // tm * tm
    pad_end = jnp.cumsum(padded)
    pad_start = pad_end - padded
    start = jnp.cumsum(counts) - counts
    dest = (pad_start[e_sorted] + jnp.arange(a_len) - start[e_sorted]).astype(jnp.int32)
    n_tiles = -(-a_len // tm) + N_EXPERTS
    p_len = n_tiles * tm
    tok_buf = jnp.full((p_len,), t_tok, jnp.int32).at[dest].set((order // TOP_K).astype(jnp.int32))
    pos = jnp.zeros((a_len,), jnp.int32).at[order].set(dest).reshape(t_tok, TOP_K)
    tile_expert = jnp.minimum(jnp.searchsorted(pad_end, jnp.arange(n_tiles) * tm, side='right'),
                              N_EXPERTS - 1).astype(jnp.int32)
    n_used = (pad_end[-1] // tm).astype(jnp.int32)[None]

    h2_flat = jnp.concatenate([h2.reshape(t_tok, d), jnp.zeros((1, d), BF16)], axis=0)
    xs = h2_flat[tok_buf]
    ys = _experts(tile_expert, n_used, xs, w_gate[0].astype(BF16), w_up[0].astype(BF16), w_down[0].astype(BF16))
    routed = jnp.sum(ys[pos].astype(F32) * wts[:, :, None], axis=1).reshape(b, s_len, d)

    return _final(x2, routed, g_f[:, None, :], norm_final_g[None, :])
```

```python
import functools
import math

import numpy as np
import jax
import jax.numpy as jnp
from jax import lax
from jax.experimental import pallas as pl
from jax.experimental.pallas import tpu as pltpu

F32 = jnp.float32
BF16 = jnp.bfloat16

GRID_W = 64
ROPE_BASE = 10000.0
EPS = 1e-6
A_HEADS = 16
A_KV_HEADS = 4
A_HEAD_DIM = 64
WINDOW = 128
B_HEADS = 8
B_NOPE_DIM = 128
B_ROPE_DIM = 64
B_QK_DIM = B_NOPE_DIM + B_ROPE_DIM
B_V_DIM = 128
Q_LORA_RANK = 512
KV_LORA_RANK = 256
N_EXPERTS = 64
TOP_K = 8
N_GROUPS = 8
TOPK_GROUPS = 4
ROUTED_SCALE = 2.5

LOG2E = math.log2(math.e)
NEG_BIG = -1e30

VMEM_LIMIT_BYTES = 56 * 1024 * 1024

ROW_TILE = 256
SWA_BLOCK = 128
MLA_TQ = 1024
MLA_TK = 768
EXPERT_TILE = 256
HEAD_PAD = 256

IN_COLS_PADDED = 2432


def _cparams(sem):
    return pltpu.CompilerParams(dimension_semantics=sem, vmem_limit_bytes=VMEM_LIMIT_BYTES)


def _const_spec(shape):
    n = len(shape)
    return pl.BlockSpec(shape, lambda *_: (0,) * n, pipeline_mode=pl.Buffered(1))


def _mod_kernel(c_ref, w_ref, b_ref, o_ref):
    c = c_ref[...]
    a = c * jax.nn.sigmoid(c)
    o_ref[...] = jnp.dot(a, w_ref[...], preferred_element_type=F32,
                         precision=lax.Precision.HIGHEST) + b_ref[...]


def _mod(c_rows, w_mod, b_mod):
    d, n = w_mod.shape
    tn = 1024
    return pl.pallas_call(
        _mod_kernel,
        grid=(n // tn,),
        in_specs=[pl.BlockSpec((8, d), lambda j: (0, 0)),
                  pl.BlockSpec((d, tn), lambda j: (0, j)),
                  pl.BlockSpec((1, tn), lambda j: (0, j))],
        out_specs=pl.BlockSpec((8, tn), lambda j: (0, j)),
        out_shape=jax.ShapeDtypeStruct((8, n), F32),
        compiler_params=_cparams(("arbitrary",)),
        name="mod",
    )(c_rows, w_mod, b_mod)


def _rms(x, g):
    var = jnp.mean(x * x, axis=-1, keepdims=True)
    return x * lax.rsqrt(var + EPS) * g


def _inproj_kernel(x_ref, ms_ref, g_ref, w_in_ref, gq_ref, w_uqt_ref, gkv_ref, w_uk_ref, w_uvt_ref,
                   rope_a_ref, rope_k_ref, rope_t_ref, *rest, sa, sb):
    qa_ref, ka_ref, va_ref, qbt_ref, kb_ref, vbt_ref = rest[-6:]
    x = x_ref[0]
    shift = ms_ref[0, 0:1, :]
    scale = ms_ref[0, 1:2, :]
    h = _rms(x, g_ref[...]) * (1.0 + scale) + shift
    z = jnp.dot(h.astype(BF16), w_in_ref[...], preferred_element_type=F32)

    cos_a = rope_a_ref[:, 0:128]
    sin_a = rope_a_ref[:, 128:256]
    for r in range(4):
        q1 = z[:, 256 * r:256 * r + 128]
        q2 = z[:, 256 * r + 128:256 * r + 256]
        qa_ref[0, :, 256 * r:256 * r + 128] = ((q1 * cos_a - q2 * sin_a) * sa).astype(BF16)
        qa_ref[0, :, 256 * r + 128:256 * r + 256] = ((q2 * cos_a + q1 * sin_a) * sa).astype(BF16)
    k1 = z[:, 1024:1152]
    k2 = z[:, 1152:1280]
    ka_ref[0, :, 0:128] = (k1 * cos_a - k2 * sin_a).astype(BF16)
    ka_ref[0, :, 128:256] = (k2 * cos_a + k1 * sin_a).astype(BF16)
    va_ref[0] = z[:, 1280:1536].astype(BF16)

    cqn = _rms(z[:, 1536:2048], gq_ref[...])
    qt = jnp.dot(w_uqt_ref[...], cqn.T.astype(BF16), preferred_element_type=F32)
    cos_t = rope_t_ref[0:32, :]
    sin_t = rope_t_ref[32:64, :]
    for hh in range(B_HEADS):
        base = HEAD_PAD * hh
        x1 = qt[base + 128:base + 160]
        x2 = qt[base + 160:base + 192]
        qbt_ref[0, hh, 0:128, :] = (qt[base:base + 128] * sb).astype(BF16)
        qbt_ref[0, hh, 128:160, :] = ((x1 * cos_t - x2 * sin_t) * sb).astype(BF16)
        qbt_ref[0, hh, 160:192, :] = ((x2 * cos_t + x1 * sin_t) * sb).astype(BF16)
        qbt_ref[0, hh, 192:256, :] = jnp.zeros((64, x.shape[0]), BF16)

    ckvn = _rms(z[:, 2048:2304], gkv_ref[...])
    knope = jnp.dot(ckvn.astype(BF16), w_uk_ref[...], preferred_element_type=F32)
    vt = jnp.dot(w_uvt_ref[...], ckvn.T.astype(BF16), preferred_element_type=F32)
    vbt_ref[0] = vt.astype(BF16)
    t = z[:, 2304:2432] * rope_k_ref[...]
    lane = lax.broadcasted_iota(jnp.int32, t.shape, 1)
    krp = jnp.where(lane < 64, t + pltpu.roll(t, 64, 1), 0.0).astype(BF16)
    for hh in range(B_HEADS):
        kb_ref[0, :, HEAD_PAD * hh:HEAD_PAD * hh + 128] = knope[:, 128 * hh:128 * hh + 128].astype(BF16)
        kb_ref[0, :, HEAD_PAD * hh + 128:HEAD_PAD * hh + 256] = krp


def _inproj(x, ms, g_attn, w_in_p, gq, w_uqt, gkv, w_uk, w_uvt, rope_a, rope_k, rope_t,
            l_total, row_block0, prev=None):
    b, n, d = x.shape
    tm = ROW_TILE
    nt = n // tm
    sa = (A_HEAD_DIM ** -0.5) * LOG2E
    sb = (B_QK_DIM ** -0.5) * LOG2E
    out_shape = [
        jax.ShapeDtypeStruct((b, l_total, A_HEADS * A_HEAD_DIM), BF16),
        jax.ShapeDtypeStruct((b, l_total, A_KV_HEADS * A_HEAD_DIM), BF16),
        jax.ShapeDtypeStruct((b, l_total, A_KV_HEADS * A_HEAD_DIM), BF16),
        jax.ShapeDtypeStruct((b, B_HEADS, HEAD_PAD, l_total), BF16),
        jax.ShapeDtypeStruct((b, l_total, B_HEADS * HEAD_PAD), BF16),
        jax.ShapeDtypeStruct((b, B_HEADS * B_V_DIM, l_total), BF16),
    ]
    r0 = row_block0
    out_specs = [
        pl.BlockSpec((1, tm, 1024), lambda bi, i: (bi, r0 + i, 0)),
        pl.BlockSpec((1, tm, 256), lambda bi, i: (bi, r0 + i, 0)),
        pl.BlockSpec((1, tm, 256), lambda bi, i: (bi, r0 + i, 0)),
        pl.BlockSpec((1, B_HEADS, HEAD_PAD, tm), lambda bi, i: (bi, 0, 0, r0 + i)),
        pl.BlockSpec((1, tm, B_HEADS * HEAD_PAD), lambda bi, i: (bi, r0 + i, 0)),
        pl.BlockSpec((1, B_HEADS * B_V_DIM, tm), lambda bi, i: (bi, 0, r0 + i)),
    ]
    in_specs = [
        pl.BlockSpec((1, tm, d), lambda bi, i: (bi, i, 0)),
        pl.BlockSpec((1, 2, d), lambda bi, i: (bi, 0, 0)),
        _const_spec((1, d)),
        _const_spec(w_in_p.shape),
        _const_spec((1, Q_LORA_RANK)),
        _const_spec(w_uqt.shape),
        _const_spec((1, KV_LORA_RANK)),
        _const_spec(w_uk.shape),
        _const_spec(w_uvt.shape),
        pl.BlockSpec((tm, 256), lambda bi, i: (r0 + i, 0)),
        pl.BlockSpec((tm, 128), lambda bi, i: (r0 + i, 0)),
        pl.BlockSpec((64, tm), lambda bi, i: (0, r0 + i)),
    ]
    args = [x, ms, g_attn, w_in_p, gq, w_uqt, gkv, w_uk, w_uvt, rope_a, rope_k, rope_t]
    aliases = {}
    if prev is not None:
        for k, p in enumerate(prev):
            in_specs.append(pl.BlockSpec(memory_space=pl.ANY))
            aliases[len(args)] = k
            args.append(p)
    return pl.pallas_call(
        functools.partial(_inproj_kernel, sa=sa, sb=sb),
        grid=(b, nt),
        in_specs=in_specs,
        out_specs=out_specs,
        out_shape=out_shape,
        input_output_aliases=aliases,
        compiler_params=_cparams(("arbitrary", "arbitrary")),
        name="inproj_ctx" if prev is not None else "inproj",
    )(*args)


def _swa_kernel(q_ref, kp_ref, kc_ref, kn_ref, vp_ref, vc_ref, vn_ref, kx_ref, vx_ref, sink_ref, o_ref, *, nblk):
    i = pl.program_id(1)
    q = q_ref[0]
    lane256 = lax.broadcasted_iota(jnp.int32, (SWA_BLOCK, 256), 1)
    zero = jnp.zeros((SWA_BLOCK, 256), BF16)
    parts = []
    for r in range(4):
        qr = q[:, 256 * r:256 * r + 256]
        for j in range(A_KV_HEADS):
            sel = (lane256 % 128) // 32 == j
            parts.append(jnp.where(sel, qr, zero))
    lhs = jnp.concatenate(parts, axis=0)
    kwin = jnp.concatenate([kp_ref[0], kc_ref[0], kn_ref[0]], axis=0)
    nt = (((1,), (1,)), ((), ()))
    s_win = lax.dot_general(lhs, kwin, nt, preferred_element_type=F32)
    s_ctx = lax.dot_general(lhs, kx_ref[0], nt, preferred_element_type=F32)

    rq = lax.broadcasted_iota(jnp.int32, (SWA_BLOCK, 3 * SWA_BLOCK), 0)
    ck = lax.broadcasted_iota(jnp.int32, (SWA_BLOCK, 3 * SWA_BLOCK), 1)
    band = jnp.abs(ck - WINDOW - rq) <= WINDOW
    band = band & ((ck >= SWA_BLOCK) | (i > 0)) & ((ck < 2 * SWA_BLOCK) | (i < nblk - 1))
    s3 = s_win.reshape(16, SWA_BLOCK, 3 * SWA_BLOCK)
    s3 = jnp.where(band[None], s3, NEG_BIG)
    s_win = s3.reshape(16 * SWA_BLOCK, 3 * SWA_BLOCK)

    sink = sink_ref[...]
    m = jnp.maximum(jnp.maximum(jnp.max(s_win, axis=-1, keepdims=True),
                                jnp.max(s_ctx, axis=-1, keepdims=True)), sink)
    p_win = jnp.exp2(s_win - m)
    p_ctx = jnp.exp2(s_ctx - m)
    den = (jnp.sum(p_win, axis=-1, keepdims=True) + jnp.sum(p_ctx, axis=-1, keepdims=True)
           + jnp.exp2(sink - m))
    vwin = jnp.concatenate([vp_ref[0], vc_ref[0], vn_ref[0]], axis=0)
    res = (jnp.dot(p_win.astype(BF16), vwin, preferred_element_type=F32)
           + jnp.dot(p_ctx.astype(BF16), vx_ref[0], preferred_element_type=F32))
    res = res * (1.0 / den)
    lane64 = lane256 // 64
    for r in range(4):
        acc = jnp.zeros((SWA_BLOCK, 256), F32)
        for j in range(A_KV_HEADS):
            s = 4 * r + j
            acc = jnp.where(lane64 == j, res[SWA_BLOCK * s:SWA_BLOCK * (s + 1)], acc)
        o_ref[0, :, 256 * r:256 * r + 256] = acc.astype(BF16)


def _swa(qa, ka, va, sink_rows, s_len, c_len):
    b = qa.shape[0]
    nblk = s_len // SWA_BLOCK
    cblk = s_len // c_len
    kv = lambda f: pl.BlockSpec((1, SWA_BLOCK, 256), f)
    prev_i = lambda bi, i: (bi, jnp.maximum(i - 1, 0), 0)
    cur_i = lambda bi, i: (bi, i, 0)
    next_i = lambda bi, i: (bi, jnp.minimum(i + 1, nblk - 1), 0)
    ctx_spec = pl.BlockSpec((1, c_len, 256), lambda bi, i: (bi, cblk, 0))
    return pl.pallas_call(
        functools.partial(_swa_kernel, nblk=nblk),
        grid=(b, nblk),
        in_specs=[pl.BlockSpec((1, SWA_BLOCK, 1024), cur_i),
                  kv(prev_i), kv(cur_i), kv(next_i), kv(prev_i), kv(cur_i), kv(next_i),
                  ctx_spec, ctx_spec, _const_spec(sink_rows.shape)],
        out_specs=pl.BlockSpec((1, SWA_BLOCK, 1024), cur_i),
        out_shape=jax.ShapeDtypeStruct((b, s_len, 1024), BF16),
        compiler_params=_cparams(("arbitrary", "arbitrary")),
        name="swa",
    )(qa, ka, ka, ka, va, va, va, ka, va, sink_rows)


def _mla_kernel(qt_ref, k_ref, vt_ref, o_ref, m_ref, l_ref, acc_ref, *, nk):
    kj = pl.program_id(2)

    @pl.when(kj == 0)
    def _():
        m_ref[...] = jnp.full(m_ref.shape, NEG_BIG, F32)
        l_ref[...] = jnp.zeros(l_ref.shape, F32)
        acc_ref[...] = jnp.zeros(acc_ref.shape, F32)

    for hh in range(B_HEADS):
        qt = qt_ref[0, hh]
        kh = k_ref[0, :, HEAD_PAD * hh:HEAD_PAD * (hh + 1)]
        st = jnp.dot(kh, qt, preferred_element_type=F32)
        m_old = m_ref[hh]
        m_new = jnp.maximum(m_old, jnp.max(st, axis=0, keepdims=True))
        alpha = jnp.exp2(m_old - m_new)
        p = jnp.exp2(st - m_new)
        l_ref[hh] = alpha * l_ref[hh] + jnp.sum(p, axis=0, keepdims=True)
        m_ref[hh] = m_new
        vth = vt_ref[0, B_V_DIM * hh:B_V_DIM * (hh + 1), :]
        pv = jnp.dot(vth, p.astype(BF16), preferred_element_type=F32)
        acc_ref[B_V_DIM * hh:B_V_DIM * (hh + 1), :] = alpha * acc_ref[B_V_DIM * hh:B_V_DIM * (hh + 1), :] + pv

    @pl.when(kj == nk - 1)
    def _():
        for hh in range(B_HEADS):
            o = acc_ref[B_V_DIM * hh:B_V_DIM * (hh + 1), :] * (1.0 / l_ref[hh])
            o_ref[0, :, B_V_DIM * hh:B_V_DIM * (hh + 1)] = o.T.astype(BF16)


def _mla(qbt, kb, vbt, s_len):
    b, _, _, l_total = qbt.shape
    tq = min(MLA_TQ, s_len)
    tk = MLA_TK if l_total % MLA_TK == 0 else 256
    nq = s_len // tq
    nk = l_total // tk
    return pl.pallas_call(
        functools.partial(_mla_kernel, nk=nk),
        grid=(b, nq, nk),
        in_specs=[pl.BlockSpec((1, B_HEADS, HEAD_PAD, tq), lambda bi, qi, kj: (bi, 0, 0, qi)),
                  pl.BlockSpec((1, tk, B_HEADS * HEAD_PAD), lambda bi, qi, kj: (bi, kj, 0)),
                  pl.BlockSpec((1, B_HEADS * B_V_DIM, tk), lambda bi, qi, kj: (bi, 0, kj))],
        out_specs=pl.BlockSpec((1, tq, B_HEADS * B_V_DIM), lambda bi, qi, kj: (bi, qi, 0)),
        out_shape=jax.ShapeDtypeStruct((b, s_len, B_HEADS * B_V_DIM), BF16),
        scratch_shapes=[pltpu.VMEM((B_HEADS, 1, tq), F32), pltpu.VMEM((B_HEADS, 1, tq), F32),
                        pltpu.VMEM((B_HEADS * B_V_DIM, tq), F32)],
        compiler_params=_cparams(("arbitrary", "arbitrary", "arbitrary")),
        name="mla",
    )(qbt, kb, vbt)


def _pack_pairs(lo, hi):
    lo_w = lax.bitcast_convert_type(lo.astype(BF16).astype(F32), jnp.uint32) >> 16
    hi_w = lax.bitcast_convert_type(hi.astype(BF16).astype(F32), jnp.uint32) & jnp.uint32(0xFFFF0000)
    return lo_w | hi_w


def _store_token_tiles(ref, v):
    n, d = v.shape
    words = _pack_pairs(v[:, :d // 2], v[:, d // 2:])
    for s in range(8):
        ref[pl.ds(s, n, stride=8), :] = words[:, 128 * s:128 * (s + 1)]


def _load_token_tiles(ref, n):
    lo, hi = [], []
    for s in range(8):
        w = ref[pl.ds(s, n, stride=8), :]
        lo.append(lax.bitcast_convert_type(w << 16, F32))
        hi.append(lax.bitcast_convert_type(w & jnp.uint32(0xFFFF0000), F32))
    return jnp.concatenate(lo, axis=1), jnp.concatenate(hi, axis=1)


def _argmax_rows(v, iota, n):
    m = jnp.max(v, axis=0, keepdims=True)
    i = jnp.min(jnp.where(v == m, iota, n), axis=0, keepdims=True)
    return m, i


def _route_tile(logits, bias_col):
    tm = logits.shape[0]
    gsz = N_EXPERTS // N_GROUPS
    s = jax.nn.sigmoid(logits.T[0:N_EXPERTS])
    sel = s + bias_col
    ninf = -jnp.inf
    g3 = sel.reshape(N_GROUPS, gsz, tm)
    w_iota = lax.broadcasted_iota(jnp.int32, g3.shape, 1).astype(F32)
    m1 = jnp.max(g3, axis=1, keepdims=True)
    i1 = jnp.min(jnp.where(g3 == m1, w_iota, float(gsz)), axis=1, keepdims=True)
    m2 = jnp.max(jnp.where(w_iota == i1, ninf, g3), axis=1, keepdims=True)
    gs = (m1 + m2).reshape(N_GROUPS, tm)
    g_iota = lax.broadcasted_iota(jnp.int32, gs.shape, 0).astype(F32)
    keep = jnp.zeros(gs.shape, F32)
    cur = gs
    for _ in range(TOPK_GROUPS):
        _, gi = _argmax_rows(cur, g_iota, float(N_GROUPS))
        hit = g_iota == gi
        keep = jnp.where(hit, 1.0, keep)
        cur = jnp.where(hit, ninf, cur)
    cur = jnp.where(keep.reshape(N_GROUPS, 1, tm) > 0.5, g3, ninf).reshape(N_EXPERTS, tm)
    e_iota = lax.broadcasted_iota(jnp.int32, cur.shape, 0).astype(F32)
    idx, wts = [], []
    hits = jnp.zeros(cur.shape, F32)
    for _ in range(TOP_K):
        _, ei = _argmax_rows(cur, e_iota, float(N_EXPERTS))
        hit = e_iota == ei
        idx.append(ei)
        wts.append(jnp.sum(jnp.where(hit, s, 0.0), axis=0, keepdims=True))
        hits = jnp.where(hit, 1.0, hits)
        cur = jnp.where(hit, ninf, cur)
    idx = jnp.concatenate(idx, axis=0)
    w = jnp.concatenate(wts, axis=0)
    w = w / jnp.sum(w, axis=0, keepdims=True) * ROUTED_SCALE
    return idx, w, hits


def _post_kernel(oa_ref, ob_ref, x_ref, mod_ref, g_ref, woa_ref, wob_ref, wr_ref, bias_ref, wsg_ref, wsu_ref,
                 wsd_ref, h2p_ref, x2_ref, idx_ref, rank_ref, w_ref, cnt_ref, run_ref):
    first = (pl.program_id(0) == 0) & (pl.program_id(1) == 0)

    @pl.when(first)
    def _():
        run_ref[...] = jnp.zeros(run_ref.shape, F32)

    att = (jnp.dot(oa_ref[0], woa_ref[...], preferred_element_type=F32)
           + jnp.dot(ob_ref[0], wob_ref[...], preferred_element_type=F32))
    g_a = mod_ref[0, 0:1, :]
    shift = mod_ref[0, 1:2, :]
    scale = mod_ref[0, 2:3, :]
    g_f = mod_ref[0, 3:4, :]
    xn = x_ref[0] + g_a * att
    h2 = _rms(xn, g_ref[...]) * (1.0 + scale) + shift
    h2b = h2.astype(BF16)
    _store_token_tiles(h2p_ref, h2)
    tm = h2.shape[0]

    logits = jnp.dot(h2, wr_ref[...], preferred_element_type=F32, precision=lax.Precision.HIGHEST)
    idx, w, hits = _route_tile(logits, bias_ref[...])
    idx_ref[...] = idx.astype(jnp.int32)
    w_ref[...] = jnp.concatenate([w, jnp.zeros((128 - TOP_K, tm), F32)], axis=0).T
    r_i = lax.broadcasted_iota(jnp.int32, (tm, tm), 0)
    c_i = lax.broadcasted_iota(jnp.int32, (tm, tm), 1)
    before = jnp.where(r_i < c_i, 1.0, 0.0).astype(BF16)
    hb = hits.astype(BF16)
    rank = jnp.dot(hb, before, preferred_element_type=F32) + run_ref[:, 0:1]
    e_iota = lax.broadcasted_iota(jnp.int32, rank.shape, 0).astype(F32)
    rows = [jnp.sum(jnp.where(e_iota == idx[k:k + 1], rank, 0.0), axis=0, keepdims=True) for k in range(TOP_K)]
    rank_ref[...] = jnp.concatenate(rows, axis=0).astype(jnp.int32)
    run_ref[...] += jnp.dot(hb, jnp.ones((tm, 128), BF16), preferred_element_type=F32)
    cnt_ref[...] = run_ref[...]

    gate = jnp.dot(h2b, wsg_ref[...], preferred_element_type=F32)
    up = jnp.dot(h2b, wsu_ref[...], preferred_element_type=F32)
    act = (gate * jax.nn.sigmoid(gate) * up).astype(BF16)
    shared = jnp.dot(act, wsd_ref[...], preferred_element_type=F32)
    x2_ref[0] = xn + g_f * shared


def _post(oa, ob, x, mod4, g_ffn, woa, wob, wr, bias_col, wsg, wsu, wsd):
    b, s_len, d = x.shape
    tm = ROW_TILE
    nt = s_len // tm
    t_tok = b * s_len
    row = lambda w: pl.BlockSpec((1, tm, w), lambda bi, i: (bi, i, 0))
    tok_t = pl.BlockSpec((TOP_K, tm), lambda bi, i: (0, bi * nt + i))
    return pl.pallas_call(
        _post_kernel,
        grid=(b, nt),
        in_specs=[row(1024), row(1024), row(d),
                  pl.BlockSpec((1, 4, d), lambda bi, i: (bi, 0, 0)),
                  _const_spec((1, d)), _const_spec(woa.shape), _const_spec(wob.shape), _const_spec(wr.shape),
                  _const_spec(bias_col.shape),
                  _const_spec(wsg.shape), _const_spec(wsu.shape), _const_spec(wsd.shape)],
        out_specs=[pl.BlockSpec((tm * 8, 128), lambda bi, i: (bi * nt + i, 0)),
                   row(d), tok_t, tok_t,
                   pl.BlockSpec((tm, 128), lambda bi, i: (bi * nt + i, 0)),
                   pl.BlockSpec((N_EXPERTS, 128), lambda bi, i: (0, 0))],
        out_shape=[jax.ShapeDtypeStruct((t_tok * 8, 128), jnp.uint32),
                   jax.ShapeDtypeStruct((b, s_len, d), F32),
                   jax.ShapeDtypeStruct((TOP_K, t_tok), jnp.int32),
                   jax.ShapeDtypeStruct((TOP_K, t_tok), jnp.int32),
                   jax.ShapeDtypeStruct((t_tok, 128), F32),
                   jax.ShapeDtypeStruct((N_EXPERTS, 128), F32)],
        scratch_shapes=[pltpu.VMEM((N_EXPERTS, 128), F32)],
        compiler_params=_cparams(("arbitrary", "arbitrary")),
        name="post",
    )(oa, ob, x, mod4, g_ffn, woa, wob, wr, bias_col, wsg, wsu, wsd)


def _plan_kernel(ps_ref, idx_ref, rank_ref, pos_ref):
    idx = idx_ref[...]
    base = jnp.zeros(idx.shape, jnp.int32)
    for e in range(N_EXPERTS):
        base = jnp.where(idx == e, ps_ref[e], base)
    pos_ref[...] = base + rank_ref[...]


def _plan(pad_start, idx_t, rank_t):
    k, t_tok = idx_t.shape
    tn = min(2048, t_tok)
    blk = lambda: pl.BlockSpec((k, tn), lambda j, ps: (0, j))
    return pl.pallas_call(
        _plan_kernel,
        grid_spec=pltpu.PrefetchScalarGridSpec(num_scalar_prefetch=1, grid=(t_tok // tn,),
                                               in_specs=[blk(), blk()], out_specs=blk()),
        out_shape=jax.ShapeDtypeStruct((k, t_tok), jnp.int32),
        compiler_params=_cparams(("arbitrary",)),
        name="plan",
    )(pad_start, idx_t, rank_t)


def _dispatch_kernel(lo_ref, hi_ref, h2p_ref, pos_hbm, xs_hbm, pos_smem, zero_ref, sem_pos, sem_rows, *, tm):
    i = pl.program_id(0)
    pos_cp = pltpu.make_async_copy(pos_hbm.at[:, pl.ds(i * tm, tm)], pos_smem, sem_pos)
    pos_cp.start()

    @pl.when(i == 0)
    def _():
        zero_ref[...] = jnp.zeros(zero_ref.shape, jnp.uint32)
        zrows = zero_ref.shape[0]

        def zero_cp(e):
            start = pl.multiple_of(hi_ref[e] * 8 - zrows, 8)
            return pltpu.make_async_copy(zero_ref, xs_hbm.at[pl.ds(start, zrows), :], sem_rows)

        def start_one(e, c):
            @pl.when(hi_ref[e] > lo_ref[e])
            def _():
                zero_cp(e).start()
            return c

        def wait_one(e, c):
            @pl.when(hi_ref[e] > lo_ref[e])
            def _():
                zero_cp(e).wait()
            return c
        lax.fori_loop(0, N_EXPERTS, start_one, 0)
        lax.fori_loop(0, N_EXPERTS, wait_one, 0)

    pos_cp.wait()

    def per_token(r, carry):
        src = h2p_ref.at[pl.ds(pl.multiple_of(r * 8, 8), 8), :]
        for k in range(TOP_K):
            p = pos_smem[k, r]
            pltpu.make_async_copy(src, xs_hbm.at[pl.ds(pl.multiple_of(p * 8, 8), 8), :], sem_rows).start()
        return carry
    lax.fori_loop(0, tm, per_token, 0)
    for k in range(TOP_K):
        pltpu.make_async_copy(h2p_ref, xs_hbm.at[pl.ds(0, tm * 8), :], sem_rows).wait()


def _dispatch(pad_lo, pad_hi, h2p, pos_t, p_len):
    t_tok = pos_t.shape[1]
    tm = ROW_TILE
    return pl.pallas_call(
        functools.partial(_dispatch_kernel, tm=tm),
        grid_spec=pltpu.PrefetchScalarGridSpec(
            num_scalar_prefetch=2, grid=(t_tok // tm,),
            in_specs=[pl.BlockSpec((tm * 8, 128), lambda i, lo, hi: (i, 0)),
                      pl.BlockSpec(memory_space=pl.ANY)],
            out_specs=pl.BlockSpec(memory_space=pl.ANY),
            scratch_shapes=[pltpu.SMEM((TOP_K, tm), jnp.int32), pltpu.VMEM((EXPERT_TILE * 8, 128), jnp.uint32),
                            pltpu.SemaphoreType.DMA, pltpu.SemaphoreType.DMA]),
        out_shape=jax.ShapeDtypeStruct((p_len * 8, 128), jnp.uint32),
        compiler_params=_cparams(("arbitrary",)),
        name="dispatch",
    )(pad_lo, pad_hi, h2p, pos_t)


def _experts_kernel(te_ref, nu_ref, x_ref, wg_ref, wu_ref, wd_ref, y_ref):
    t = pl.program_id(0)

    @pl.when(t < nu_ref[0])
    def _():
        tm = x_ref.shape[0] // 8
        lo, hi = _load_token_tiles(x_ref, tm)
        xb = jnp.concatenate([lo, hi], axis=1).astype(BF16)
        gate = jnp.dot(xb, wg_ref[0], preferred_element_type=F32)
        up = jnp.dot(xb, wu_ref[0], preferred_element_type=F32)
        act = (gate * jax.nn.sigmoid(gate) * up).astype(BF16)
        _store_token_tiles(y_ref, jnp.dot(act, wd_ref[0], preferred_element_type=F32))


def _experts(tile_expert, n_used, xs, wg, wu, wd):
    p_len = xs.shape[0] // 8
    d, ff = wg.shape[1], wg.shape[2]
    tm = EXPERT_TILE
    n_tiles = p_len // tm
    rows = pl.BlockSpec((tm * 8, 128), lambda t, te, nu: (jnp.minimum(t, nu[0] - 1), 0))
    grid_spec = pltpu.PrefetchScalarGridSpec(
        num_scalar_prefetch=2,
        grid=(n_tiles,),
        in_specs=[rows,
                  pl.BlockSpec((1, d, ff), lambda t, te, nu: (te[t], 0, 0)),
                  pl.BlockSpec((1, d, ff), lambda t, te, nu: (te[t], 0, 0)),
                  pl.BlockSpec((1, ff, d), lambda t, te, nu: (te[t], 0, 0))],
        out_specs=rows,
    )
    return pl.pallas_call(
        _experts_kernel,
        grid_spec=grid_spec,
        out_shape=jax.ShapeDtypeStruct((p_len * 8, 128), jnp.uint32),
        compiler_params=_cparams(("arbitrary",)),
        name="experts",
    )(tile_expert, n_used, xs, wg, wu, wd)


def _final_kernel(x2_ref, w_ref, gf_ref, g_ref, pos_hbm, ys_hbm, o_ref, pos_smem, buf_ref, sem_pos, sem_rows, *, tm):
    i = pl.program_id(0)
    pos_cp = pltpu.make_async_copy(pos_hbm.at[:, pl.ds(i * tm, tm)], pos_smem, sem_pos)
    pos_cp.start()
    pos_cp.wait()

    def per_token(r, carry):
        for k in range(TOP_K):
            p = pos_smem[k, r]
            pltpu.make_async_copy(ys_hbm.at[pl.ds(pl.multiple_of(p * 8, 8), 8), :],
                                  buf_ref.at[k, pl.ds(pl.multiple_of(r * 8, 8), 8), :], sem_rows).start()
        return carry
    lax.fori_loop(0, tm, per_token, 0)
    for k in range(TOP_K):
        pltpu.make_async_copy(ys_hbm.at[pl.ds(0, tm * 8), :], buf_ref.at[k], sem_rows).wait()

    w = w_ref[...]
    d = x2_ref.shape[1]
    lo_acc = jnp.zeros((tm, d // 2), F32)
    hi_acc = jnp.zeros((tm, d // 2), F32)
    for k in range(TOP_K):
        lo, hi = _load_token_tiles(buf_ref.at[k], tm)
        lo_acc = lo_acc + w[:, k:k + 1] * lo
        hi_acc = hi_acc + w[:, k:k + 1] * hi
    routed = jnp.concatenate([lo_acc, hi_acc], axis=1)
    x = x2_ref[...] + gf_ref[0] * routed
    o_ref[...] = _rms(x, g_ref[...])


def _final(x2, w128, g_f, g_final, pos_t, ys):
    b, s_len, d = x2.shape
    t_tok = b * s_len
    tm = 128
    nt = s_len // tm
    row = pl.BlockSpec((tm, d), lambda i: (i, 0))
    out = pl.pallas_call(
        functools.partial(_final_kernel, tm=tm),
        grid=(t_tok // tm,),
        in_specs=[row, pl.BlockSpec((tm, 128), lambda i: (i, 0)),
                  pl.BlockSpec((1, 1, d), lambda i: (i // nt, 0, 0)), _const_spec((1, d)),
                  pl.BlockSpec(memory_space=pl.ANY), pl.BlockSpec(memory_space=pl.ANY)],
        out_specs=row,
        out_shape=jax.ShapeDtypeStruct((t_tok, d), F32),
        scratch_shapes=[pltpu.SMEM((TOP_K, tm), jnp.int32), pltpu.VMEM((TOP_K, tm * 8, 128), jnp.uint32),
                        pltpu.SemaphoreType.DMA, pltpu.SemaphoreType.DMA],
        compiler_params=_cparams(("arbitrary",)),
        name="final",
    )(x2.reshape(t_tok, d), w128, g_f, g_final, pos_t, ys)
    return out.reshape(b, s_len, d)


def _rope_tables(s_len, c_len):
    rows = s_len // GRID_W
    r = jnp.repeat(jnp.arange(rows, dtype=F32), GRID_W)
    cidx = jnp.tile(jnp.arange(GRID_W, dtype=F32), rows)
    n_freq = 16
    inv = ROPE_BASE ** (-jnp.arange(n_freq, dtype=F32) / n_freq)
    ang = jnp.concatenate([r[:, None] * inv, cidx[:, None] * inv], axis=-1)
    cos = jnp.concatenate([jnp.cos(ang), jnp.ones((c_len, 32), F32)], axis=0)
    sin = jnp.concatenate([jnp.sin(ang), jnp.zeros((c_len, 32), F32)], axis=0)
    rope_a = jnp.concatenate([jnp.tile(cos, (1, 4)), jnp.tile(sin, (1, 4))], axis=1)
    rope_k = jnp.concatenate([cos, cos, sin, sin], axis=1)
    rope_t = jnp.concatenate([cos.T, sin.T], axis=0)
    return rope_a, rope_k, rope_t


def _in_weight(w_in):
    qa_cols = np.empty((4, 2, 4, 32), np.int32)
    for r in range(4):
        for p in range(2):
            for j in range(4):
                qa_cols[r, p, j] = (4 * j + r) * 64 + p * 32 + np.arange(32)
    ka_cols = np.empty((2, 4, 32), np.int32)
    for p in range(2):
        for j in range(4):
            ka_cols[p, j] = 1024 + j * 64 + p * 32 + np.arange(32)
    cols = np.concatenate([qa_cols.reshape(-1), ka_cols.reshape(-1), np.arange(1280, 2368)])
    w = w_in[:, cols]
    kr = w_in[:, 2304:2368]
    kr_rot = jnp.concatenate([-kr[:, 32:], kr[:, :32]], axis=1)
    return jnp.concatenate([w, kr_rot], axis=1).astype(BF16)


def kernel(x, c, ctx, c_ctx, w_mod, b_mod, norm_attn_g, norm_ffn_g, w_in, attn_sink, q_a_norm_g, w_uq,
           kv_a_norm_g, w_ukv, w_out, w_router, router_bias, w_gate, w_up, w_down, ws_gate, ws_up, ws_down,
           norm_final_g):
    b, s_len, d = x.shape
    c_len = ctx.shape[1]
    l_total = s_len + c_len
    t_tok = b * s_len

    c_rows = jnp.zeros((8, d), F32).at[:b].set(c).at[b].set(c_ctx)
    mod = _mod(c_rows, w_mod[0], b_mod[0][None, :])
    mod6 = mod.reshape(8, 6, d)
    sh_a, sc_a, g_a, sh_f, sc_f, g_f = (mod6[:b, k] for k in range(6))
    ms_x = jnp.stack([sh_a, sc_a], axis=1)
    ms_c = jnp.broadcast_to(jnp.stack([mod6[b, 0], mod6[b, 1]], axis=0)[None], (b, 2, d))

    w_in_p = _in_weight(w_in[0])
    w_uq_h = w_uq[0].reshape(Q_LORA_RANK, B_HEADS, B_QK_DIM)
    w_uqt = jnp.pad(jnp.transpose(w_uq_h, (1, 2, 0)), ((0, 0), (0, HEAD_PAD - B_QK_DIM), (0, 0)))
    w_uqt = w_uqt.reshape(B_HEADS * HEAD_PAD, Q_LORA_RANK).astype(BF16)
    w_ukv_h = w_ukv[0].reshape(KV_LORA_RANK, B_HEADS, B_NOPE_DIM + B_V_DIM)
    w_uk = w_ukv_h[:, :, :B_NOPE_DIM].reshape(KV_LORA_RANK, B_HEADS * B_NOPE_DIM).astype(BF16)
    w_uvt = w_ukv_h[:, :, B_NOPE_DIM:].reshape(KV_LORA_RANK, B_HEADS * B_V_DIM).T.astype(BF16)
    rope_a, rope_k, rope_t = _rope_tables(s_len, c_len)

    g_attn = norm_attn_g[0][None, :]
    gq = q_a_norm_g[0][None, :]
    gkv = kv_a_norm_g[0][None, :]
    proj_w = (g_attn, w_in_p, gq, w_uqt, gkv, w_uk, w_uvt, rope_a, rope_k, rope_t)
    outs = _inproj(x, ms_x, *proj_w, l_total, 0)
    qa, ka, va, qbt, kb, vbt = _inproj(ctx, ms_c, *proj_w, l_total, s_len // ROW_TILE, prev=outs)

    stack_heads = np.array([4 * (s % 4) + s // 4 for s in range(16)], np.int32)
    sink_rows = jnp.repeat(attn_sink[0][stack_heads] * LOG2E, SWA_BLOCK)[:, None].astype(F32)
    o_a = _swa(qa, ka, va, sink_rows, s_len, c_len)
    o_b = _mla(qbt, kb, vbt, s_len)

    oa_rows = np.empty((4, 4, 64), np.int32)
    for r in range(4):
        for j in range(4):
            oa_rows[r, j] = (4 * j + r) * 64 + np.arange(64)
    woa = w_out[0][oa_rows.reshape(-1)].astype(BF16)
    wob = w_out[0][1024:].astype(BF16)
    wr = jnp.pad(w_router[0], ((0, 0), (0, 128 - N_EXPERTS)))
    mod4 = jnp.stack([g_a, sh_f, sc_f, g_f], axis=1)
    h2p, x2, idx_t, rank_t, w128, cnt = _post(
        o_a, o_b, x, mod4, norm_ffn_g[0][None, :], woa, wob, wr, router_bias[0][:, None].astype(F32),
        ws_gate[0].astype(BF16), ws_up[0].astype(BF16), ws_down[0].astype(BF16))

    tm = EXPERT_TILE
    counts = cnt[:, 0].astype(jnp.int32)
    padded = (counts + tm - 1) // tm * tm
    pad_end = jnp.cumsum(padded).astype(jnp.int32)
    pad_start = pad_end - padded
    n_tiles = -(-t_tok * TOP_K // tm) + N_EXPERTS
    tile_expert = jnp.minimum(jnp.searchsorted(pad_end, jnp.arange(n_tiles, dtype=jnp.int32) * tm, side='right'),
                              N_EXPERTS - 1).astype(jnp.int32)
    n_used = (pad_end[-1] // tm).astype(jnp.int32)[None]

    pos_t = _plan(pad_start, idx_t, rank_t)
    xs = _dispatch(pad_start, pad_end, h2p, pos_t, n_tiles * tm)
    ys = _experts(tile_expert, n_used, xs, w_gate[0].astype(BF16), w_up[0].astype(BF16), w_down[0].astype(BF16))
    return _final(x2, w128, g_f[:, None, :], norm_final_g[None, :], pos_t, ys)
```

```python
import functools
import math

import numpy as np
import jax
import jax.numpy as jnp
from jax import lax
from jax.experimental import pallas as pl
from jax.experimental.pallas import tpu as pltpu

F32 = jnp.float32
BF16 = jnp.bfloat16

GRID_W = 64
ROPE_BASE = 10000.0
EPS = 1e-6
A_HEADS = 16
A_KV_HEADS = 4
A_HEAD_DIM = 64
WINDOW = 128
B_HEADS = 8
B_NOPE_DIM = 128
B_ROPE_DIM = 64
B_QK_DIM = B_NOPE_DIM + B_ROPE_DIM
B_V_DIM = 128
Q_LORA_RANK = 512
KV_LORA_RANK = 256
N_EXPERTS = 64
TOP_K = 8
N_GROUPS = 8
TOPK_GROUPS = 4
ROUTED_SCALE = 2.5

LOG2E = math.log2(math.e)
NEG_BIG = -1e30

VMEM_LIMIT_BYTES = 56 * 1024 * 1024

ROW_TILE = 256
SWA_BLOCK = 128
SWA_SUB = 1
MLA_TQ = 1024
MLA_TK = 768
EXPERT_TILE = 256
HEAD_PAD = 256

IN_COLS_PADDED = 2432


def _cparams(sem):
    return pltpu.CompilerParams(dimension_semantics=sem, vmem_limit_bytes=VMEM_LIMIT_BYTES)


def _const_spec(shape):
    n = len(shape)
    return pl.BlockSpec(shape, lambda *_: (0,) * n, pipeline_mode=pl.Buffered(1))


def _mod_kernel(c_ref, w_ref, b_ref, o_ref):
    c = c_ref[...]
    a = c * jax.nn.sigmoid(c)
    o_ref[...] = jnp.dot(a, w_ref[...], preferred_element_type=F32,
                         precision=lax.Precision.HIGHEST) + b_ref[...]


def _mod(c_rows, w_mod, b_mod):
    d, n = w_mod.shape
    tn = 1024
    return pl.pallas_call(
        _mod_kernel,
        grid=(n // tn,),
        in_specs=[pl.BlockSpec((8, d), lambda j: (0, 0)),
                  pl.BlockSpec((d, tn), lambda j: (0, j)),
                  pl.BlockSpec((1, tn), lambda j: (0, j))],
        out_specs=pl.BlockSpec((8, tn), lambda j: (0, j)),
        out_shape=jax.ShapeDtypeStruct((8, n), F32),
        compiler_params=_cparams(("arbitrary",)),
        name="mod",
    )(c_rows, w_mod, b_mod)


def _rms(x, g):
    var = jnp.mean(x * x, axis=-1, keepdims=True)
    return x * lax.rsqrt(var + EPS) * g


def _inproj_kernel(x_ref, ms_ref, g_ref, w_in_ref, gq_ref, w_uqt_ref, gkv_ref, w_uk_ref, w_uvt_ref,
                   rope_a_ref, rope_k_ref, rope_t_ref, *rest, sa, sb):
    qa_ref, ka_ref, va_ref, qbt_ref, kb_ref, vbt_ref = rest[-6:]
    x = x_ref[0]
    shift = ms_ref[0, 0:1, :]
    scale = ms_ref[0, 1:2, :]
    h = _rms(x, g_ref[...]) * (1.0 + scale) + shift
    z = jnp.dot(h.astype(BF16), w_in_ref[...], preferred_element_type=F32)

    cos_a = rope_a_ref[:, 0:128]
    sin_a = rope_a_ref[:, 128:256]
    for r in range(4):
        q1 = z[:, 256 * r:256 * r + 128]
        q2 = z[:, 256 * r + 128:256 * r + 256]
        qa_ref[0, :, 256 * r:256 * r + 128] = ((q1 * cos_a - q2 * sin_a) * sa).astype(BF16)
        qa_ref[0, :, 256 * r + 128:256 * r + 256] = ((q2 * cos_a + q1 * sin_a) * sa).astype(BF16)
    k1 = z[:, 1024:1152]
    k2 = z[:, 1152:1280]
    ka_ref[0, :, 0:128] = (k1 * cos_a - k2 * sin_a).astype(BF16)
    ka_ref[0, :, 128:256] = (k2 * cos_a + k1 * sin_a).astype(BF16)
    va_ref[0] = z[:, 1280:1536].astype(BF16)

    cqn = _rms(z[:, 1536:2048], gq_ref[...])
    qt = jnp.dot(w_uqt_ref[...], cqn.T.astype(BF16), preferred_element_type=F32)
    cos_t = rope_t_ref[0:32, :]
    sin_t = rope_t_ref[32:64, :]
    for hh in range(B_HEADS):
        base = HEAD_PAD * hh
        x1 = qt[base + 128:base + 160]
        x2 = qt[base + 160:base + 192]
        qbt_ref[0, hh, 0:128, :] = (qt[base:base + 128] * sb).astype(BF16)
        qbt_ref[0, hh, 128:160, :] = ((x1 * cos_t - x2 * sin_t) * sb).astype(BF16)
        qbt_ref[0, hh, 160:192, :] = ((x2 * cos_t + x1 * sin_t) * sb).astype(BF16)
        qbt_ref[0, hh, 192:256, :] = jnp.zeros((64, x.shape[0]), BF16)

    ckvn = _rms(z[:, 2048:2304], gkv_ref[...])
    knope = jnp.dot(ckvn.astype(BF16), w_uk_ref[...], preferred_element_type=F32)
    vt = jnp.dot(w_uvt_ref[...], ckvn.T.astype(BF16), preferred_element_type=F32)
    vbt_ref[0] = vt.astype(BF16)
    t = z[:, 2304:2432] * rope_k_ref[...]
    lane = lax.broadcasted_iota(jnp.int32, t.shape, 1)
    krp = jnp.where(lane < 64, t + pltpu.roll(t, 64, 1), 0.0).astype(BF16)
    for hh in range(B_HEADS):
        kb_ref[0, :, HEAD_PAD * hh:HEAD_PAD * hh + 128] = knope[:, 128 * hh:128 * hh + 128].astype(BF16)
        kb_ref[0, :, HEAD_PAD * hh + 128:HEAD_PAD * hh + 256] = krp


def _inproj(x, ms, g_attn, w_in_p, gq, w_uqt, gkv, w_uk, w_uvt, rope_a, rope_k, rope_t,
            l_total, row_block0, prev=None):
    b, n, d = x.shape
    tm = ROW_TILE
    nt = n // tm
    sa = (A_HEAD_DIM ** -0.5) * LOG2E
    sb = (B_QK_DIM ** -0.5) * LOG2E
    out_shape = [
        jax.ShapeDtypeStruct((b, l_total, A_HEADS * A_HEAD_DIM), BF16),
        jax.ShapeDtypeStruct((b, l_total, A_KV_HEADS * A_HEAD_DIM), BF16),
        jax.ShapeDtypeStruct((b, l_total, A_KV_HEADS * A_HEAD_DIM), BF16),
        jax.ShapeDtypeStruct((b, B_HEADS, HEAD_PAD, l_total), BF16),
        jax.ShapeDtypeStruct((b, l_total, B_HEADS * HEAD_PAD), BF16),
        jax.ShapeDtypeStruct((b, B_HEADS * B_V_DIM, l_total), BF16),
    ]
    r0 = row_block0
    out_specs = [
        pl.BlockSpec((1, tm, 1024), lambda bi, i: (bi, r0 + i, 0)),
        pl.BlockSpec((1, tm, 256), lambda bi, i: (bi, r0 + i, 0)),
        pl.BlockSpec((1, tm, 256), lambda bi, i: (bi, r0 + i, 0)),
        pl.BlockSpec((1, B_HEADS, HEAD_PAD, tm), lambda bi, i: (bi, 0, 0, r0 + i)),
        pl.BlockSpec((1, tm, B_HEADS * HEAD_PAD), lambda bi, i: (bi, r0 + i, 0)),
        pl.BlockSpec((1, B_HEADS * B_V_DIM, tm), lambda bi, i: (bi, 0, r0 + i)),
    ]
    in_specs = [
        pl.BlockSpec((1, tm, d), lambda bi, i: (bi, i, 0)),
        pl.BlockSpec((1, 2, d), lambda bi, i: (bi, 0, 0)),
        _const_spec((1, d)),
        _const_spec(w_in_p.shape),
        _const_spec((1, Q_LORA_RANK)),
        _const_spec(w_uqt.shape),
        _const_spec((1, KV_LORA_RANK)),
        _const_spec(w_uk.shape),
        _const_spec(w_uvt.shape),
        pl.BlockSpec((tm, 256), lambda bi, i: (r0 + i, 0)),
        pl.BlockSpec((tm, 128), lambda bi, i: (r0 + i, 0)),
        pl.BlockSpec((64, tm), lambda bi, i: (0, r0 + i)),
    ]
    args = [x, ms, g_attn, w_in_p, gq, w_uqt, gkv, w_uk, w_uvt, rope_a, rope_k, rope_t]
    aliases = {}
    if prev is not None:
        for k, p in enumerate(prev):
            in_specs.append(pl.BlockSpec(memory_space=pl.ANY))
            aliases[len(args)] = k
            args.append(p)
    return pl.pallas_call(
        functools.partial(_inproj_kernel, sa=sa, sb=sb),
        grid=(b, nt),
        in_specs=in_specs,
        out_specs=out_specs,
        out_shape=out_shape,
        input_output_aliases=aliases,
        compiler_params=_cparams(("arbitrary", "arbitrary")),
        name="inproj_ctx" if prev is not None else "inproj",
    )(*args)


def _swa_block(q, kwin, vwin, kx, vx, sink, first, last):
    lane256 = lax.broadcasted_iota(jnp.int32, (SWA_BLOCK, 256), 1)
    zero = jnp.zeros((SWA_BLOCK, 256), BF16)
    parts = []
    for r in range(4):
        qr = q[:, 256 * r:256 * r + 256]
        for j in range(A_KV_HEADS):
            sel = (lane256 % 128) // 32 == j
            parts.append(jnp.where(sel, qr, zero))
    lhs = jnp.concatenate(parts, axis=0)
    nt = (((1,), (1,)), ((), ()))
    s_win = lax.dot_general(lhs, kwin, nt, preferred_element_type=F32)
    s_ctx = lax.dot_general(lhs, kx, nt, preferred_element_type=F32)

    rq = lax.broadcasted_iota(jnp.int32, (SWA_BLOCK, 3 * SWA_BLOCK), 0)
    ck = lax.broadcasted_iota(jnp.int32, (SWA_BLOCK, 3 * SWA_BLOCK), 1)
    band = jnp.abs(ck - WINDOW - rq) <= WINDOW
    band = band & ((ck >= SWA_BLOCK) | jnp.logical_not(first)) & ((ck < 2 * SWA_BLOCK) | jnp.logical_not(last))
    s3 = s_win.reshape(16, SWA_BLOCK, 3 * SWA_BLOCK)
    s3 = jnp.where(band[None], s3, NEG_BIG)
    s_win = s3.reshape(16 * SWA_BLOCK, 3 * SWA_BLOCK)

    m = jnp.maximum(jnp.maximum(jnp.max(s_win, axis=-1, keepdims=True),
                                jnp.max(s_ctx, axis=-1, keepdims=True)), sink)
    p_win = jnp.exp2(s_win - m)
    p_ctx = jnp.exp2(s_ctx - m)
    den = (jnp.sum(p_win, axis=-1, keepdims=True) + jnp.sum(p_ctx, axis=-1, keepdims=True)
           + jnp.exp2(sink - m))
    res = (jnp.dot(p_win.astype(BF16), vwin, preferred_element_type=F32)
           + jnp.dot(p_ctx.astype(BF16), vx, preferred_element_type=F32))
    res = res * (1.0 / den)
    lane64 = lane256 // 64
    cols = []
    for r in range(4):
        acc = jnp.zeros((SWA_BLOCK, 256), F32)
        for j in range(A_KV_HEADS):
            s = 4 * r + j
            acc = jnp.where(lane64 == j, res[SWA_BLOCK * s:SWA_BLOCK * (s + 1)], acc)
        cols.append(acc)
    return jnp.concatenate(cols, axis=1)


def _swa_kernel(q_ref, kp_ref, kc_ref, kn_ref, vp_ref, vc_ref, vn_ref, kx_ref, vx_ref, sink_ref, o_ref, *, nsteps):
    i = pl.program_id(1)
    kall = jnp.concatenate([kp_ref[0], kc_ref[0], kn_ref[0]], axis=0)
    vall = jnp.concatenate([vp_ref[0], vc_ref[0], vn_ref[0]], axis=0)
    sink = sink_ref[...]
    for sub in range(SWA_SUB):
        rows = slice(SWA_BLOCK * sub, SWA_BLOCK * (sub + 1))
        win = slice(SWA_BLOCK * sub, SWA_BLOCK * (sub + 3))
        first = (i == 0) if sub == 0 else jnp.bool_(False)
        last = (i == nsteps - 1) if sub == SWA_SUB - 1 else jnp.bool_(False)
        o = _swa_block(q_ref[0, rows, :], kall[win], vall[win], kx_ref[0], vx_ref[0], sink, first, last)
        o_ref[0, rows, :] = o.astype(BF16)


def _swa(qa, ka, va, sink_rows, s_len, c_len):
    b = qa.shape[0]
    rows = SWA_SUB * SWA_BLOCK
    nsteps = s_len // rows
    nblk = s_len // SWA_BLOCK
    edge = lambda f: pl.BlockSpec((1, SWA_BLOCK, 256), f)
    prev_i = lambda bi, i: (bi, jnp.maximum(SWA_SUB * i - 1, 0), 0)
    next_i = lambda bi, i: (bi, jnp.minimum(SWA_SUB * (i + 1), nblk - 1), 0)
    cur_i = lambda bi, i: (bi, i, 0)
    cur = pl.BlockSpec((1, rows, 256), cur_i)
    ctx_spec = pl.BlockSpec((1, c_len, 256), lambda bi, i: (bi, s_len // c_len, 0))
    return pl.pallas_call(
        functools.partial(_swa_kernel, nsteps=nsteps),
        grid=(b, nsteps),
        in_specs=[pl.BlockSpec((1, rows, 1024), cur_i),
                  edge(prev_i), cur, edge(next_i), edge(prev_i), cur, edge(next_i),
                  ctx_spec, ctx_spec, _const_spec(sink_rows.shape)],
        out_specs=pl.BlockSpec((1, rows, 1024), cur_i),
        out_shape=jax.ShapeDtypeStruct((b, s_len, 1024), BF16),
        compiler_params=_cparams(("arbitrary", "arbitrary")),
        name="swa",
    )(qa, ka, ka, ka, va, va, va, ka, va, sink_rows)


def _mla_kernel(qt_ref, k_ref, vt_ref, o_ref, m_ref, l_ref, acc_ref, *, nk):
    kj = pl.program_id(2)

    @pl.when(kj == 0)
    def _():
        m_ref[...] = jnp.full(m_ref.shape, NEG_BIG, F32)
        l_ref[...] = jnp.zeros(l_ref.shape, F32)
        acc_ref[...] = jnp.zeros(acc_ref.shape, F32)

    for hh in range(B_HEADS):
        qt = qt_ref[0, hh]
        kh = k_ref[0, :, HEAD_PAD * hh:HEAD_PAD * (hh + 1)]
        st = jnp.dot(kh, qt, preferred_element_type=F32)
        m_old = m_ref[hh]
        m_new = jnp.maximum(m_old, jnp.max(st, axis=0, keepdims=True))
        alpha = jnp.exp2(m_old - m_new)
        p = jnp.exp2(st - m_new)
        l_ref[hh] = alpha * l_ref[hh] + jnp.sum(p, axis=0, keepdims=True)
        m_ref[hh] = m_new
        vth = vt_ref[0, B_V_DIM * hh:B_V_DIM * (hh + 1), :]
        pv = jnp.dot(vth, p.astype(BF16), preferred_element_type=F32)
        acc_ref[B_V_DIM * hh:B_V_DIM * (hh + 1), :] = alpha * acc_ref[B_V_DIM * hh:B_V_DIM * (hh + 1), :] + pv

    @pl.when(kj == nk - 1)
    def _():
        for hh in range(B_HEADS):
            o = acc_ref[B_V_DIM * hh:B_V_DIM * (hh + 1), :] * (1.0 / l_ref[hh])
            o_ref[0, :, B_V_DIM * hh:B_V_DIM * (hh + 1)] = o.T.astype(BF16)


def _mla(qbt, kb, vbt, s_len):
    b, _, _, l_total = qbt.shape
    tq = min(MLA_TQ, s_len)
    tk = MLA_TK if l_total % MLA_TK == 0 else 256
    nq = s_len // tq
    nk = l_total // tk
    return pl.pallas_call(
        functools.partial(_mla_kernel, nk=nk),
        grid=(b, nq, nk),
        in_specs=[pl.BlockSpec((1, B_HEADS, HEAD_PAD, tq), lambda bi, qi, kj: (bi, 0, 0, qi)),
                  pl.BlockSpec((1, tk, B_HEADS * HEAD_PAD), lambda bi, qi, kj: (bi, kj, 0)),
                  pl.BlockSpec((1, B_HEADS * B_V_DIM, tk), lambda bi, qi, kj: (bi, 0, kj))],
        out_specs=pl.BlockSpec((1, tq, B_HEADS * B_V_DIM), lambda bi, qi, kj: (bi, qi, 0)),
        out_shape=jax.ShapeDtypeStruct((b, s_len, B_HEADS * B_V_DIM), BF16),
        scratch_shapes=[pltpu.VMEM((B_HEADS, 1, tq), F32), pltpu.VMEM((B_HEADS, 1, tq), F32),
                        pltpu.VMEM((B_HEADS * B_V_DIM, tq), F32)],
        compiler_params=_cparams(("arbitrary", "arbitrary", "arbitrary")),
        name="mla",
    )(qbt, kb, vbt)


def _pack_pairs(lo, hi):
    lo_w = lax.bitcast_convert_type(lo.astype(BF16).astype(F32), jnp.uint32) >> 16
    hi_w = lax.bitcast_convert_type(hi.astype(BF16).astype(F32), jnp.uint32) & jnp.uint32(0xFFFF0000)
    return lo_w | hi_w


def _store_token_tiles(ref, v):
    n, d = v.shape
    words = _pack_pairs(v[:, :d // 2], v[:, d // 2:])
    for s in range(8):
        ref[pl.ds(s, n, stride=8), :] = words[:, 128 * s:128 * (s + 1)]


def _load_token_tiles(ref, n):
    lo, hi = [], []
    for s in range(8):
        w = ref[pl.ds(s, n, stride=8), :]
        lo.append(lax.bitcast_convert_type(w << 16, F32))
        hi.append(lax.bitcast_convert_type(w & jnp.uint32(0xFFFF0000), F32))
    return jnp.concatenate(lo, axis=1), jnp.concatenate(hi, axis=1)


def _argmax_rows(v, iota, n):
    m = jnp.max(v, axis=0, keepdims=True)
    i = jnp.min(jnp.where(v == m, iota, n), axis=0, keepdims=True)
    return m, i


def _route_tile(logits, bias_col):
    tm = logits.shape[0]
    gsz = N_EXPERTS // N_GROUPS
    s = jax.nn.sigmoid(logits.T[0:N_EXPERTS])
    sel = s + bias_col
    ninf = -jnp.inf
    g3 = sel.reshape(N_GROUPS, gsz, tm)
    w_iota = lax.broadcasted_iota(jnp.int32, g3.shape, 1).astype(F32)
    m1 = jnp.max(g3, axis=1, keepdims=True)
    i1 = jnp.min(jnp.where(g3 == m1, w_iota, float(gsz)), axis=1, keepdims=True)
    m2 = jnp.max(jnp.where(w_iota == i1, ninf, g3), axis=1, keepdims=True)
    gs = (m1 + m2).reshape(N_GROUPS, tm)
    g_iota = lax.broadcasted_iota(jnp.int32, gs.shape, 0).astype(F32)
    keep = jnp.zeros(gs.shape, F32)
    cur = gs
    for _ in range(TOPK_GROUPS):
        _, gi = _argmax_rows(cur, g_iota, float(N_GROUPS))
        hit = g_iota == gi
        keep = jnp.where(hit, 1.0, keep)
        cur = jnp.where(hit, ninf, cur)
    cur = jnp.where(keep.reshape(N_GROUPS, 1, tm) > 0.5, g3, ninf).reshape(N_EXPERTS, tm)
    e_iota = lax.broadcasted_iota(jnp.int32, cur.shape, 0).astype(F32)
    idx, wts = [], []
    hits = jnp.zeros(cur.shape, F32)
    for _ in range(TOP_K):
        _, ei = _argmax_rows(cur, e_iota, float(N_EXPERTS))
        hit = e_iota == ei
        idx.append(ei)
        wts.append(jnp.sum(jnp.where(hit, s, 0.0), axis=0, keepdims=True))
        hits = jnp.where(hit, 1.0, hits)
        cur = jnp.where(hit, ninf, cur)
    idx = jnp.concatenate(idx, axis=0)
    w = jnp.concatenate(wts, axis=0)
    w = w / jnp.sum(w, axis=0, keepdims=True) * ROUTED_SCALE
    return idx, w, hits


def _post_kernel(oa_ref, ob_ref, x_ref, mod_ref, g_ref, woa_ref, wob_ref, wr_ref, bias_ref, wsg_ref, wsu_ref,
                 wsd_ref, h2p_ref, x2_ref, idx_ref, rank_ref, w_ref, cnt_ref, run_ref):
    first = (pl.program_id(0) == 0) & (pl.program_id(1) == 0)

    @pl.when(first)
    def _():
        run_ref[...] = jnp.zeros(run_ref.shape, F32)

    att = (jnp.dot(oa_ref[0], woa_ref[...], preferred_element_type=F32)
           + jnp.dot(ob_ref[0], wob_ref[...], preferred_element_type=F32))
    g_a = mod_ref[0, 0:1, :]
    shift = mod_ref[0, 1:2, :]
    scale = mod_ref[0, 2:3, :]
    g_f = mod_ref[0, 3:4, :]
    xn = x_ref[0] + g_a * att
    h2 = _rms(xn, g_ref[...]) * (1.0 + scale) + shift
    h2b = h2.astype(BF16)
    _store_token_tiles(h2p_ref, h2)
    tm = h2.shape[0]

    h2_lo = (h2 - h2b.astype(F32)).astype(BF16)
    logits = (jnp.dot(h2b, wr_ref[0], preferred_element_type=F32)
              + jnp.dot(h2_lo, wr_ref[0], preferred_element_type=F32)
              + jnp.dot(h2b, wr_ref[1], preferred_element_type=F32))
    idx, w, hits = _route_tile(logits, bias_ref[...])
    idx_ref[...] = idx.astype(jnp.int32)
    w_ref[...] = jnp.concatenate([w, jnp.zeros((128 - TOP_K, tm), F32)], axis=0).T
    r_i = lax.broadcasted_iota(jnp.int32, (tm, tm), 0)
    c_i = lax.broadcasted_iota(jnp.int32, (tm, tm), 1)
    before = jnp.where(r_i < c_i, 1.0, 0.0).astype(BF16)
    hb = hits.astype(BF16)
    rank = jnp.dot(hb, before, preferred_element_type=F32) + run_ref[:, 0:1]
    e_iota = lax.broadcasted_iota(jnp.int32, rank.shape, 0).astype(F32)
    rows = [jnp.sum(jnp.where(e_iota == idx[k:k + 1], rank, 0.0), axis=0, keepdims=True) for k in range(TOP_K)]
    rank_ref[...] = jnp.concatenate(rows, axis=0).astype(jnp.int32)
    run_ref[...] += jnp.dot(hb, jnp.ones((tm, 128), BF16), preferred_element_type=F32)
    cnt_ref[...] = run_ref[...]

    gate = jnp.dot(h2b, wsg_ref[...], preferred_element_type=F32)
    up = jnp.dot(h2b, wsu_ref[...], preferred_element_type=F32)
    act = (gate * jax.nn.sigmoid(gate) * up).astype(BF16)
    shared = jnp.dot(act, wsd_ref[...], preferred_element_type=F32)
    x2_ref[0] = xn + g_f * shared


def _post(oa, ob, x, mod4, g_ffn, woa, wob, wr, bias_col, wsg, wsu, wsd):
    b, s_len, d = x.shape
    tm = ROW_TILE
    nt = s_len // tm
    t_tok = b * s_len
    row = lambda w: pl.BlockSpec((1, tm, w), lambda bi, i: (bi, i, 0))
    tok_t = pl.BlockSpec((TOP_K, tm), lambda bi, i: (0, bi * nt + i))
    return pl.pallas_call(
        _post_kernel,
        grid=(b, nt),
        in_specs=[row(1024), row(1024), row(d),
                  pl.BlockSpec((1, 4, d), lambda bi, i: (bi, 0, 0)),
                  _const_spec((1, d)), _const_spec(woa.shape), _const_spec(wob.shape), _const_spec(wr.shape),
                  _const_spec(bias_col.shape),
                  _const_spec(wsg.shape), _const_spec(wsu.shape), _const_spec(wsd.shape)],
        out_specs=[pl.BlockSpec((tm * 8, 128), lambda bi, i: (bi * nt + i, 0)),
                   row(d), tok_t, tok_t,
                   pl.BlockSpec((tm, 128), lambda bi, i: (bi * nt + i, 0)),
                   pl.BlockSpec((N_EXPERTS, 128), lambda bi, i: (0, 0))],
        out_shape=[jax.ShapeDtypeStruct((t_tok * 8, 128), jnp.uint32),
                   jax.ShapeDtypeStruct((b, s_len, d), F32),
                   jax.ShapeDtypeStruct((TOP_K, t_tok), jnp.int32),
                   jax.ShapeDtypeStruct((TOP_K, t_tok), jnp.int32),
                   jax.ShapeDtypeStruct((t_tok, 128), F32),
                   jax.ShapeDtypeStruct((N_EXPERTS, 128), F32)],
        scratch_shapes=[pltpu.VMEM((N_EXPERTS, 128), F32)],
        compiler_params=_cparams(("arbitrary", "arbitrary")),
        name="post",
    )(oa, ob, x, mod4, g_ffn, woa, wob, wr, bias_col, wsg, wsu, wsd)


def _plan_kernel(ps_ref, idx_ref, rank_ref, pos_ref):
    idx = idx_ref[...]
    base = jnp.zeros(idx.shape, jnp.int32)
    for e in range(N_EXPERTS):
        base = jnp.where(idx == e, ps_ref[e], base)
    pos_ref[...] = base + rank_ref[...]


def _plan(pad_start, idx_t, rank_t):
    k, t_tok = idx_t.shape
    tn = min(2048, t_tok)
    blk = lambda: pl.BlockSpec((k, tn), lambda j, ps: (0, j))
    return pl.pallas_call(
        _plan_kernel,
        grid_spec=pltpu.PrefetchScalarGridSpec(num_scalar_prefetch=1, grid=(t_tok // tn,),
                                               in_specs=[blk(), blk()], out_specs=blk()),
        out_shape=jax.ShapeDtypeStruct((k, t_tok), jnp.int32),
        compiler_params=_cparams(("arbitrary",)),
        name="plan",
    )(pad_start, idx_t, rank_t)


def _dispatch_kernel(lo_ref, hi_ref, h2p_ref, pos_hbm, xs_hbm, pos_smem, zero_ref, sem_pos, sem_rows, *, tm):
    i = pl.program_id(0)
    pos_cp = pltpu.make_async_copy(pos_hbm.at[:, pl.ds(i * tm, tm)], pos_smem, sem_pos)
    pos_cp.start()

    @pl.when(i == 0)
    def _():
        zero_ref[...] = jnp.zeros(zero_ref.shape, jnp.uint32)
        zrows = zero_ref.shape[0]

        def zero_cp(e):
            start = pl.multiple_of(hi_ref[e] * 8 - zrows, 8)
            return pltpu.make_async_copy(zero_ref, xs_hbm.at[pl.ds(start, zrows), :], sem_rows)

        def start_one(e, c):
            @pl.when(hi_ref[e] > lo_ref[e])
            def _():
                zero_cp(e).start()
            return c

        def wait_one(e, c):
            @pl.when(hi_ref[e] > lo_ref[e])
            def _():
                zero_cp(e).wait()
            return c
        lax.fori_loop(0, N_EXPERTS, start_one, 0)
        lax.fori_loop(0, N_EXPERTS, wait_one, 0)

    pos_cp.wait()

    def per_token(r, carry):
        src = h2p_ref.at[pl.ds(pl.multiple_of(r * 8, 8), 8), :]
        for k in range(TOP_K):
            p = pos_smem[k, r]
            pltpu.make_async_copy(src, xs_hbm.at[pl.ds(pl.multiple_of(p * 8, 8), 8), :],
                                  sem_rows).start(priority=k % 2)
        return carry
    lax.fori_loop(0, tm, per_token, 0)
    for k in range(TOP_K):
        pltpu.make_async_copy(h2p_ref, xs_hbm.at[pl.ds(0, tm * 8), :], sem_rows).wait()


def _dispatch(pad_lo, pad_hi, h2p, pos_t, p_len):
    t_tok = pos_t.shape[1]
    tm = ROW_TILE
    return pl.pallas_call(
        functools.partial(_dispatch_kernel, tm=tm),
        grid_spec=pltpu.PrefetchScalarGridSpec(
            num_scalar_prefetch=2, grid=(t_tok // tm,),
            in_specs=[pl.BlockSpec((tm * 8, 128), lambda i, lo, hi: (i, 0)),
                      pl.BlockSpec(memory_space=pl.ANY)],
            out_specs=pl.BlockSpec(memory_space=pl.ANY),
            scratch_shapes=[pltpu.SMEM((TOP_K, tm), jnp.int32), pltpu.VMEM((EXPERT_TILE * 8, 128), jnp.uint32),
                            pltpu.SemaphoreType.DMA, pltpu.SemaphoreType.DMA]),
        out_shape=jax.ShapeDtypeStruct((p_len * 8, 128), jnp.uint32),
        compiler_params=_cparams(("arbitrary",)),
        name="dispatch",
    )(pad_lo, pad_hi, h2p, pos_t)


def _experts_kernel(te_ref, nu_ref, x_ref, wg_ref, wu_ref, wd_ref, y_ref):
    t = pl.program_id(0)

    @pl.when(t < nu_ref[0])
    def _():
        tm = x_ref.shape[0] // 8
        lo, hi = _load_token_tiles(x_ref, tm)
        xb = jnp.concatenate([lo, hi], axis=1).astype(BF16)
        gate = jnp.dot(xb, wg_ref[0], preferred_element_type=F32)
        up = jnp.dot(xb, wu_ref[0], preferred_element_type=F32)
        act = (gate * jax.nn.sigmoid(gate) * up).astype(BF16)
        _store_token_tiles(y_ref, jnp.dot(act, wd_ref[0], preferred_element_type=F32))


def _experts(tile_expert, n_used, xs, wg, wu, wd):
    p_len = xs.shape[0] // 8
    d, ff = wg.shape[1], wg.shape[2]
    tm = EXPERT_TILE
    n_tiles = p_len // tm
    rows = pl.BlockSpec((tm * 8, 128), lambda t, te, nu: (jnp.minimum(t, nu[0] - 1), 0))
    grid_spec = pltpu.PrefetchScalarGridSpec(
        num_scalar_prefetch=2,
        grid=(n_tiles,),
        in_specs=[rows,
                  pl.BlockSpec((1, d, ff), lambda t, te, nu: (te[t], 0, 0)),
                  pl.BlockSpec((1, d, ff), lambda t, te, nu: (te[t], 0, 0)),
                  pl.BlockSpec((1, ff, d), lambda t, te, nu: (te[t], 0, 0))],
        out_specs=rows,
    )
    return pl.pallas_call(
        _experts_kernel,
        grid_spec=grid_spec,
        out_shape=jax.ShapeDtypeStruct((p_len * 8, 128), jnp.uint32),
        compiler_params=_cparams(("arbitrary",)),
        name="experts",
    )(tile_expert, n_used, xs, wg, wu, wd)


def _final_kernel(x2_ref, w_ref, gf_ref, g_ref, pos_hbm, ys_hbm, o_ref, pos_smem, buf_ref, sem_pos, sem_rows,
                  *, tm, nsteps):
    i = pl.program_id(0)
    slot = i % 2

    def pos_copy(step, s):
        return pltpu.make_async_copy(pos_hbm.at[:, pl.ds(step * tm, tm)], pos_smem.at[s], sem_pos.at[s])

    def issue_gathers(s):
        def per_token(r, carry):
            for k in range(TOP_K):
                p = pos_smem[s, k, r]
                pltpu.make_async_copy(ys_hbm.at[pl.ds(pl.multiple_of(p * 8, 8), 8), :],
                                      buf_ref.at[s, k, pl.ds(pl.multiple_of(r * 8, 8), 8), :],
                                      sem_rows.at[s]).start(priority=k % 2)
            return carry
        lax.fori_loop(0, tm, per_token, 0)

    @pl.when(i == 0)
    def _():
        pos_copy(0, 0).start()
        pos_copy(0, 0).wait()
        issue_gathers(0)
        if nsteps > 1:
            pos_copy(1, 1).start()

    @pl.when(i + 1 < nsteps)
    def _():
        pos_copy(i + 1, 1 - slot).wait()
        issue_gathers(1 - slot)

    @pl.when(i + 2 < nsteps)
    def _():
        pos_copy(i + 2, slot).start()

    for k in range(TOP_K):
        pltpu.make_async_copy(ys_hbm.at[pl.ds(0, tm * 8), :], buf_ref.at[slot, k], sem_rows.at[slot]).wait()

    w = w_ref[...]
    d = x2_ref.shape[1]
    lo_acc = jnp.zeros((tm, d // 2), F32)
    hi_acc = jnp.zeros((tm, d // 2), F32)
    for k in range(TOP_K):
        lo, hi = _load_token_tiles(buf_ref.at[slot, k], tm)
        lo_acc = lo_acc + w[:, k:k + 1] * lo
        hi_acc = hi_acc + w[:, k:k + 1] * hi
    routed = jnp.concatenate([lo_acc, hi_acc], axis=1)
    x = x2_ref[...] + gf_ref[0] * routed
    o_ref[...] = _rms(x, g_ref[...])


def _final(x2, w128, g_f, g_final, pos_t, ys):
    b, s_len, d = x2.shape
    t_tok = b * s_len
    tm = 128
    nt = s_len // tm
    nsteps = t_tok // tm
    row = pl.BlockSpec((tm, d), lambda i: (i, 0))
    out = pl.pallas_call(
        functools.partial(_final_kernel, tm=tm, nsteps=nsteps),
        grid=(nsteps,),
        in_specs=[row, pl.BlockSpec((tm, 128), lambda i: (i, 0)),
                  pl.BlockSpec((1, 1, d), lambda i: (i // nt, 0, 0)), _const_spec((1, d)),
                  pl.BlockSpec(memory_space=pl.ANY), pl.BlockSpec(memory_space=pl.ANY)],
        out_specs=row,
        out_shape=jax.ShapeDtypeStruct((t_tok, d), F32),
        scratch_shapes=[pltpu.SMEM((2, TOP_K, tm), jnp.int32), pltpu.VMEM((2, TOP_K, tm * 8, 128), jnp.uint32),
                        pltpu.SemaphoreType.DMA((2,)), pltpu.SemaphoreType.DMA((2,))],
        compiler_params=_cparams(("arbitrary",)),
        name="final",
    )(x2.reshape(t_tok, d), w128, g_f, g_final, pos_t, ys)
    return out.reshape(b, s_len, d)


def _rope_tables(s_len, c_len):
    rows = s_len // GRID_W
    r = jnp.repeat(jnp.arange(rows, dtype=F32), GRID_W)
    cidx = jnp.tile(jnp.arange(GRID_W, dtype=F32), rows)
    n_freq = 16
    inv = ROPE_BASE ** (-jnp.arange(n_freq, dtype=F32) / n_freq)
    ang = jnp.concatenate([r[:, None] * inv, cidx[:, None] * inv], axis=-1)
    cos = jnp.concatenate([jnp.cos(ang), jnp.ones((c_len, 32), F32)], axis=0)
    sin = jnp.concatenate([jnp.sin(ang), jnp.zeros((c_len, 32), F32)], axis=0)
    rope_a = jnp.concatenate([jnp.tile(cos, (1, 4)), jnp.tile(sin, (1, 4))], axis=1)
    rope_k = jnp.concatenate([cos, cos, sin, sin], axis=1)
    rope_t = jnp.concatenate([cos.T, sin.T], axis=0)
    return rope_a, rope_k, rope_t


def _in_weight(w_in):
    qa_cols = np.empty((4, 2, 4, 32), np.int32)
    for r in range(4):
        for p in range(2):
            for j in range(4):
                qa_cols[r, p, j] = (4 * j + r) * 64 + p * 32 + np.arange(32)
    ka_cols = np.empty((2, 4, 32), np.int32)
    for p in range(2):
        for j in range(4):
            ka_cols[p, j] = 1024 + j * 64 + p * 32 + np.arange(32)
    cols = np.concatenate([qa_cols.reshape(-1), ka_cols.reshape(-1), np.arange(1280, 2368)])
    w = w_in[:, cols]
    kr = w_in[:, 2304:2368]
    kr_rot = jnp.concatenate([-kr[:, 32:], kr[:, :32]], axis=1)
    return jnp.concatenate([w, kr_rot], axis=1).astype(BF16)


def kernel(x, c, ctx, c_ctx, w_mod, b_mod, norm_attn_g, norm_ffn_g, w_in, attn_sink, q_a_norm_g, w_uq,
           kv_a_norm_g, w_ukv, w_out, w_router, router_bias, w_gate, w_up, w_down, ws_gate, ws_up, ws_down,
           norm_final_g):
    b, s_len, d = x.shape
    c_len = ctx.shape[1]
    l_total = s_len + c_len
    t_tok = b * s_len

    c_rows = jnp.zeros((8, d), F32).at[:b].set(c).at[b].set(c_ctx)
    mod = _mod(c_rows, w_mod[0], b_mod[0][None, :])
    mod6 = mod.reshape(8, 6, d)
    sh_a, sc_a, g_a, sh_f, sc_f, g_f = (mod6[:b, k] for k in range(6))
    ms_x = jnp.stack([sh_a, sc_a], axis=1)
    ms_c = jnp.broadcast_to(jnp.stack([mod6[b, 0], mod6[b, 1]], axis=0)[None], (b, 2, d))

    w_in_p = _in_weight(w_in[0])
    w_uq_h = w_uq[0].reshape(Q_LORA_RANK, B_HEADS, B_QK_DIM)
    w_uqt = jnp.pad(jnp.transpose(w_uq_h, (1, 2, 0)), ((0, 0), (0, HEAD_PAD - B_QK_DIM), (0, 0)))
    w_uqt = w_uqt.reshape(B_HEADS * HEAD_PAD, Q_LORA_RANK).astype(BF16)
    w_ukv_h = w_ukv[0].reshape(KV_LORA_RANK, B_HEADS, B_NOPE_DIM + B_V_DIM)
    w_uk = w_ukv_h[:, :, :B_NOPE_DIM].reshape(KV_LORA_RANK, B_HEADS * B_NOPE_DIM).astype(BF16)
    w_uvt = w_ukv_h[:, :, B_NOPE_DIM:].reshape(KV_LORA_RANK, B_HEADS * B_V_DIM).T.astype(BF16)
    rope_a, rope_k, rope_t = _rope_tables(s_len, c_len)

    g_attn = norm_attn_g[0][None, :]
    gq = q_a_norm_g[0][None, :]
    gkv = kv_a_norm_g[0][None, :]
    proj_w = (g_attn, w_in_p, gq, w_uqt, gkv, w_uk, w_uvt, rope_a, rope_k, rope_t)
    outs = _inproj(x, ms_x, *proj_w, l_total, 0)
    qa, ka, va, qbt, kb, vbt = _inproj(ctx, ms_c, *proj_w, l_total, s_len // ROW_TILE, prev=outs)

    stack_heads = np.array([4 * (s % 4) + s // 4 for s in range(16)], np.int32)
    sink_rows = jnp.repeat(attn_sink[0][stack_heads] * LOG2E, SWA_BLOCK)[:, None].astype(F32)
    o_a = _swa(qa, ka, va, sink_rows, s_len, c_len)
    o_b = _mla(qbt, kb, vbt, s_len)

    oa_rows = np.empty((4, 4, 64), np.int32)
    for r in range(4):
        for j in range(4):
            oa_rows[r, j] = (4 * j + r) * 64 + np.arange(64)
    woa = w_out[0][oa_rows.reshape(-1)].astype(BF16)
    wob = w_out[0][1024:].astype(BF16)
    wr = jnp.pad(w_router[0], ((0, 0), (0, 128 - N_EXPERTS)))
    wr_hi = wr.astype(BF16)
    wr = jnp.stack([wr_hi, (wr - wr_hi.astype(F32)).astype(BF16)], axis=0)
    mod4 = jnp.stack([g_a, sh_f, sc_f, g_f], axis=1)
    h2p, x2, idx_t, rank_t, w128, cnt = _post(
        o_a, o_b, x, mod4, norm_ffn_g[0][None, :], woa, wob, wr, router_bias[0][:, None].astype(F32),
        ws_gate[0].astype(BF16), ws_up[0].astype(BF16), ws_down[0].astype(BF16))

    tm = EXPERT_TILE
    counts = cnt[:, 0].astype(jnp.int32)
    padded = (counts + tm - 1) // tm * tm
    pad_end = jnp.cumsum(padded).astype(jnp.int32)
    pad_start = pad_end - padded
    n_tiles = -(-t_tok * TOP_K // tm) + N_EXPERTS
    tile_row0 = jnp.arange(n_tiles, dtype=jnp.int32) * tm
    tile_expert = jnp.minimum(jnp.sum((pad_end[None, :] <= tile_row0[:, None]).astype(jnp.int32), axis=1),
                              N_EXPERTS - 1)
    n_used = (pad_end[-1] // tm).astype(jnp.int32)[None]

    pos_t = _plan(pad_start, idx_t, rank_t)
    xs = _dispatch(pad_start, pad_end, h2p, pos_t, n_tiles * tm)
    ys = _experts(tile_expert, n_used, xs, w_gate[0].astype(BF16), w_up[0].astype(BF16), w_down[0].astype(BF16))
    return _final(x2, w128, g_f[:, None, :], norm_final_g[None, :], pos_t, ys)
```

```python
import functools
import math

import numpy as np
import jax
import jax.numpy as jnp
from jax import lax
from jax.experimental import pallas as pl
from jax.experimental.pallas import tpu as pltpu

F32 = jnp.float32
BF16 = jnp.bfloat16

GRID_W = 64
ROPE_BASE = 10000.0
EPS = 1e-6
A_HEADS = 16
A_KV_HEADS = 4
A_HEAD_DIM = 64
WINDOW = 128
B_HEADS = 8
B_NOPE_DIM = 128
B_ROPE_DIM = 64
B_QK_DIM = B_NOPE_DIM + B_ROPE_DIM
B_V_DIM = 128
Q_LORA_RANK = 512
KV_LORA_RANK = 256
N_EXPERTS = 64
TOP_K = 8
N_GROUPS = 8
TOPK_GROUPS = 4
ROUTED_SCALE = 2.5

LOG2E = math.log2(math.e)
NEG_BIG = -1e30

VMEM_LIMIT_BYTES = 56 * 1024 * 1024

ROW_TILE = 256
SWA_BLOCK = 128
SWA_SUB = 1
MLA_TQ = 1024
MLA_TK = 768
MLA_KC = 128
EXPERT_TILE = 256
HEAD_PAD = 256

IN_COLS_PADDED = 2432


def _cparams(sem):
    return pltpu.CompilerParams(dimension_semantics=sem, vmem_limit_bytes=VMEM_LIMIT_BYTES)


def _const_spec(shape):
    n = len(shape)
    return pl.BlockSpec(shape, lambda *_: (0,) * n, pipeline_mode=pl.Buffered(1))


def _mod_kernel(c_ref, w_ref, b_ref, o_ref):
    c = c_ref[...]
    a = c * jax.nn.sigmoid(c)
    o_ref[...] = jnp.dot(a, w_ref[...], preferred_element_type=F32,
                         precision=lax.Precision.HIGHEST) + b_ref[...]


def _mod(c_rows, w_mod, b_mod):
    d, n = w_mod.shape
    tn = 1024
    return pl.pallas_call(
        _mod_kernel,
        grid=(n // tn,),
        in_specs=[pl.BlockSpec((8, d), lambda j: (0, 0)),
                  pl.BlockSpec((d, tn), lambda j: (0, j)),
                  pl.BlockSpec((1, tn), lambda j: (0, j))],
        out_specs=pl.BlockSpec((8, tn), lambda j: (0, j)),
        out_shape=jax.ShapeDtypeStruct((8, n), F32),
        compiler_params=_cparams(("arbitrary",)),
        name="mod",
    )(c_rows, w_mod, b_mod)


def _rms(x, g):
    var = jnp.mean(x * x, axis=-1, keepdims=True)
    return x * lax.rsqrt(var + EPS) * g


def _inproj_kernel(x_ref, ms_ref, g_ref, w_in_ref, gq_ref, w_uqt_ref, gkv_ref, w_uk_ref, w_uvt_ref,
                   rope_a_ref, rope_k_ref, rope_t_ref, *rest, sa, sb):
    qa_ref, ka_ref, va_ref, qbt_ref, kb_ref, vbt_ref = rest[-6:]
    x = x_ref[0]
    shift = ms_ref[0, 0:1, :]
    scale = ms_ref[0, 1:2, :]
    h = _rms(x, g_ref[...]) * (1.0 + scale) + shift
    z = jnp.dot(h.astype(BF16), w_in_ref[...], preferred_element_type=F32)

    cos_a = rope_a_ref[:, 0:128]
    sin_a = rope_a_ref[:, 128:256]
    for r in range(4):
        q1 = z[:, 256 * r:256 * r + 128]
        q2 = z[:, 256 * r + 128:256 * r + 256]
        qa_ref[0, :, 256 * r:256 * r + 128] = ((q1 * cos_a - q2 * sin_a) * sa).astype(BF16)
        qa_ref[0, :, 256 * r + 128:256 * r + 256] = ((q2 * cos_a + q1 * sin_a) * sa).astype(BF16)
    k1 = z[:, 1024:1152]
    k2 = z[:, 1152:1280]
    ka_ref[0, :, 0:128] = (k1 * cos_a - k2 * sin_a).astype(BF16)
    ka_ref[0, :, 128:256] = (k2 * cos_a + k1 * sin_a).astype(BF16)
    va_ref[0] = z[:, 1280:1536].astype(BF16)

    cqn = _rms(z[:, 1536:2048], gq_ref[...])
    qt = jnp.dot(w_uqt_ref[...], cqn.T.astype(BF16), preferred_element_type=F32)
    cos_t = rope_t_ref[0:32, :]
    sin_t = rope_t_ref[32:64, :]
    for hh in range(B_HEADS):
        base = HEAD_PAD * hh
        x1 = qt[base + 128:base + 160]
        x2 = qt[base + 160:base + 192]
        qbt_ref[0, hh, 0, 0:128, :] = (qt[base:base + 128] * sb).astype(BF16)
        qbt_ref[0, hh, 0, 128:160, :] = ((x1 * cos_t - x2 * sin_t) * sb).astype(BF16)
        qbt_ref[0, hh, 0, 160:192, :] = ((x2 * cos_t + x1 * sin_t) * sb).astype(BF16)
        qbt_ref[0, hh, 0, 192:256, :] = jnp.zeros((64, x.shape[0]), BF16)

    ckvn = _rms(z[:, 2048:2304], gkv_ref[...])
    knope = jnp.dot(ckvn.astype(BF16), w_uk_ref[...], preferred_element_type=F32)
    vt = jnp.dot(w_uvt_ref[...], ckvn.T.astype(BF16), preferred_element_type=F32)
    t = z[:, 2304:2432] * rope_k_ref[...]
    lane = lax.broadcasted_iota(jnp.int32, t.shape, 1)
    krp = jnp.where(lane < 64, t + pltpu.roll(t, 64, 1), 0.0).astype(BF16)
    for hh in range(B_HEADS):
        vbt_ref[0, hh] = vt[B_V_DIM * hh:B_V_DIM * (hh + 1)].astype(BF16)
        kb_ref[0, hh, :, 0:128] = knope[:, 128 * hh:128 * hh + 128].astype(BF16)
        kb_ref[0, hh, :, 128:256] = krp


def _inproj(x, ms, g_attn, w_in_p, gq, w_uqt, gkv, w_uk, w_uvt, rope_a, rope_k, rope_t,
            l_total, row_block0, prev=None):
    b, n, d = x.shape
    tm = ROW_TILE
    nt = n // tm
    sa = (A_HEAD_DIM ** -0.5) * LOG2E
    sb = (B_QK_DIM ** -0.5) * LOG2E
    out_shape = [
        jax.ShapeDtypeStruct((b, l_total, A_HEADS * A_HEAD_DIM), BF16),
        jax.ShapeDtypeStruct((b, l_total, A_KV_HEADS * A_HEAD_DIM), BF16),
        jax.ShapeDtypeStruct((b, l_total, A_KV_HEADS * A_HEAD_DIM), BF16),
        jax.ShapeDtypeStruct((b, B_HEADS, l_total // tm, HEAD_PAD, tm), BF16),
        jax.ShapeDtypeStruct((b, B_HEADS, l_total, HEAD_PAD), BF16),
        jax.ShapeDtypeStruct((b, B_HEADS, B_V_DIM, l_total), BF16),
    ]
    r0 = row_block0
    out_specs = [
        pl.BlockSpec((1, tm, 1024), lambda bi, i: (bi, r0 + i, 0)),
        pl.BlockSpec((1, tm, 256), lambda bi, i: (bi, r0 + i, 0)),
        pl.BlockSpec((1, tm, 256), lambda bi, i: (bi, r0 + i, 0)),
        pl.BlockSpec((1, B_HEADS, 1, HEAD_PAD, tm), lambda bi, i: (bi, 0, r0 + i, 0, 0)),
        pl.BlockSpec((1, B_HEADS, tm, HEAD_PAD), lambda bi, i: (bi, 0, r0 + i, 0)),
        pl.BlockSpec((1, B_HEADS, B_V_DIM, tm), lambda bi, i: (bi, 0, 0, r0 + i)),
    ]
    in_specs = [
        pl.BlockSpec((1, tm, d), lambda bi, i: (bi, i, 0)),
        pl.BlockSpec((1, 2, d), lambda bi, i: (bi, 0, 0)),
        _const_spec((1, d)),
        _const_spec(w_in_p.shape),
        _const_spec((1, Q_LORA_RANK)),
        _const_spec(w_uqt.shape),
        _const_spec((1, KV_LORA_RANK)),
        _const_spec(w_uk.shape),
        _const_spec(w_uvt.shape),
        pl.BlockSpec((tm, 256), lambda bi, i: (r0 + i, 0)),
        pl.BlockSpec((tm, 128), lambda bi, i: (r0 + i, 0)),
        pl.BlockSpec((64, tm), lambda bi, i: (0, r0 + i)),
    ]
    args = [x, ms, g_attn, w_in_p, gq, w_uqt, gkv, w_uk, w_uvt, rope_a, rope_k, rope_t]
    aliases = {}
    if prev is not None:
        for k, p in enumerate(prev):
            in_specs.append(pl.BlockSpec(memory_space=pl.ANY))
            aliases[len(args)] = k
            args.append(p)
    return pl.pallas_call(
        functools.partial(_inproj_kernel, sa=sa, sb=sb),
        grid=(b, nt),
        in_specs=in_specs,
        out_specs=out_specs,
        out_shape=out_shape,
        input_output_aliases=aliases,
        compiler_params=_cparams(("arbitrary", "arbitrary")),
        name="inproj_ctx" if prev is not None else "inproj",
    )(*args)


def _swa_block(q, kwin, vwin, kx, vx, sink, first, last):
    lane256 = lax.broadcasted_iota(jnp.int32, (SWA_BLOCK, 256), 1)
    zero = jnp.zeros((SWA_BLOCK, 256), BF16)
    parts = []
    for r in range(4):
        qr = q[:, 256 * r:256 * r + 256]
        for j in range(A_KV_HEADS):
            sel = (lane256 % 128) // 32 == j
            parts.append(jnp.where(sel, qr, zero))
    lhs = jnp.concatenate(parts, axis=0)
    nt = (((1,), (1,)), ((), ()))
    s_win = lax.dot_general(lhs, kwin, nt, preferred_element_type=F32)
    s_ctx = lax.dot_general(lhs, kx, nt, preferred_element_type=F32)

    rq = lax.broadcasted_iota(jnp.int32, (SWA_BLOCK, 3 * SWA_BLOCK), 0)
    ck = lax.broadcasted_iota(jnp.int32, (SWA_BLOCK, 3 * SWA_BLOCK), 1)
    band = jnp.abs(ck - WINDOW - rq) <= WINDOW
    band = band & ((ck >= SWA_BLOCK) | jnp.logical_not(first)) & ((ck < 2 * SWA_BLOCK) | jnp.logical_not(last))
    s3 = s_win.reshape(16, SWA_BLOCK, 3 * SWA_BLOCK)
    s3 = jnp.where(band[None], s3, NEG_BIG)
    s_win = s3.reshape(16 * SWA_BLOCK, 3 * SWA_BLOCK)

    m = jnp.maximum(jnp.maximum(jnp.max(s_win, axis=-1, keepdims=True),
                                jnp.max(s_ctx, axis=-1, keepdims=True)), sink)
    p_win = jnp.exp2(s_win - m)
    p_ctx = jnp.exp2(s_ctx - m)
    den = (jnp.sum(p_win, axis=-1, keepdims=True) + jnp.sum(p_ctx, axis=-1, keepdims=True)
           + jnp.exp2(sink - m))
    res = (jnp.dot(p_win.astype(BF16), vwin, preferred_element_type=F32)
           + jnp.dot(p_ctx.astype(BF16), vx, preferred_element_type=F32))
    res = res * (1.0 / den)
    lane64 = lane256 // 64
    cols = []
    for r in range(4):
        acc = jnp.zeros((SWA_BLOCK, 256), F32)
        for j in range(A_KV_HEADS):
            s = 4 * r + j
            acc = jnp.where(lane64 == j, res[SWA_BLOCK * s:SWA_BLOCK * (s + 1)], acc)
        cols.append(acc)
    return jnp.concatenate(cols, axis=1)


def _swa_kernel(q_ref, kp_ref, kc_ref, kn_ref, vp_ref, vc_ref, vn_ref, kx_ref, vx_ref, sink_ref, o_ref, *, nsteps):
    i = pl.program_id(1)
    kall = jnp.concatenate([kp_ref[0], kc_ref[0], kn_ref[0]], axis=0)
    vall = jnp.concatenate([vp_ref[0], vc_ref[0], vn_ref[0]], axis=0)
    sink = sink_ref[...]
    for sub in range(SWA_SUB):
        rows = slice(SWA_BLOCK * sub, SWA_BLOCK * (sub + 1))
        win = slice(SWA_BLOCK * sub, SWA_BLOCK * (sub + 3))
        first = (i == 0) if sub == 0 else jnp.bool_(False)
        last = (i == nsteps - 1) if sub == SWA_SUB - 1 else jnp.bool_(False)
        o = _swa_block(q_ref[0, rows, :], kall[win], vall[win], kx_ref[0], vx_ref[0], sink, first, last)
        o_ref[0, rows, :] = o.astype(BF16)


def _swa(qa, ka, va, sink_rows, s_len, c_len):
    b = qa.shape[0]
    rows = SWA_SUB * SWA_BLOCK
    nsteps = s_len // rows
    nblk = s_len // SWA_BLOCK
    edge = lambda f: pl.BlockSpec((1, SWA_BLOCK, 256), f)
    prev_i = lambda bi, i: (bi, jnp.maximum(SWA_SUB * i - 1, 0), 0)
    next_i = lambda bi, i: (bi, jnp.minimum(SWA_SUB * (i + 1), nblk - 1), 0)
    cur_i = lambda bi, i: (bi, i, 0)
    cur = pl.BlockSpec((1, rows, 256), cur_i)
    ctx_spec = pl.BlockSpec((1, c_len, 256), lambda bi, i: (bi, s_len // c_len, 0))
    return pl.pallas_call(
        functools.partial(_swa_kernel, nsteps=nsteps),
        grid=(b, nsteps),
        in_specs=[pl.BlockSpec((1, rows, 1024), cur_i),
                  edge(prev_i), cur, edge(next_i), edge(prev_i), cur, edge(next_i),
                  ctx_spec, ctx_spec, _const_spec(sink_rows.shape)],
        out_specs=pl.BlockSpec((1, rows, 1024), cur_i),
        out_shape=jax.ShapeDtypeStruct((b, s_len, 1024), BF16),
        compiler_params=_cparams(("arbitrary", "arbitrary")),
        name="swa",
    )(qa, ka, ka, ka, va, va, va, ka, va, sink_rows)


def _mla_kernel(qt_ref, k_ref, vt_ref, o_ref, m_ref, l_ref, acc_ref, s0, s1, bm0, bm1, p0, p1, a0, a1, *, nk, nc):
    kj = pl.program_id(2)
    tk = k_ref.shape[2]
    qc = qt_ref.shape[4]
    n_it = B_HEADS * nc
    s_refs, bm_refs, p_refs, a_refs = (s0, s1), (bm0, bm1), (p0, p1), (a0, a1)

    @pl.when(kj == 0)
    def _():
        m_ref[...] = jnp.full(m_ref.shape, NEG_BIG, F32)
        l_ref[...] = jnp.zeros(l_ref.shape, F32)
        acc_ref[...] = jnp.zeros(acc_ref.shape, F32)

    def scores(i):
        h, c = divmod(i, nc)
        st = jnp.dot(k_ref[0, h], qt_ref[0, h, c], preferred_element_type=F32)
        s_refs[i % 2][...] = st
        bm_refs[i % 2][...] = jnp.max(st, axis=0, keepdims=True)

    def softmax(i):
        h, c = divmod(i, nc)
        m_old = m_ref[h, c]
        m_new = jnp.maximum(m_old, bm_refs[i % 2][...])
        alpha = jnp.exp2(m_old - m_new)
        m_ref[h, c] = m_new
        a_refs[i % 2][...] = alpha
        lsum = jnp.zeros((1, qc), F32)
        for r in range(tk // MLA_KC):
            rows = slice(MLA_KC * r, MLA_KC * (r + 1))
            p = jnp.exp2(s_refs[i % 2][rows, :] - m_new)
            lsum = lsum + jnp.sum(p, axis=0, keepdims=True)
            p_refs[i % 2][rows, :] = p.astype(BF16)
        l_ref[h, c] = alpha * l_ref[h, c] + lsum

    def values(i):
        h, c = divmod(i, nc)
        pv = jnp.dot(vt_ref[0, h], p_refs[i % 2][...], preferred_element_type=F32)
        acc_ref[h, c] = a_refs[i % 2][...] * acc_ref[h, c] + pv

    scores(0)
    for i in range(n_it):
        if i + 1 < n_it:
            scores(i + 1)
        if i >= 1:
            values(i - 1)
        softmax(i)
    values(n_it - 1)

    @pl.when(kj == nk - 1)
    def _():
        for hh in range(B_HEADS):
            for c in range(nc):
                o = acc_ref[hh, c] * (1.0 / l_ref[hh, c])
                o_ref[0, qc * c:qc * (c + 1), B_V_DIM * hh:B_V_DIM * (hh + 1)] = o.T.astype(BF16)


def _mla(qbt, kb, vbt, s_len):
    b, _, _, _, qc = qbt.shape
    l_total = kb.shape[2]
    tq = min(MLA_TQ, s_len)
    tk = MLA_TK if l_total % MLA_TK == 0 else 256
    nc = tq // qc
    nq = s_len // tq
    nk = l_total // tk
    two = lambda shape, dt: [pltpu.VMEM(shape, dt), pltpu.VMEM(shape, dt)]
    return pl.pallas_call(
        functools.partial(_mla_kernel, nk=nk, nc=nc),
        grid=(b, nq, nk),
        in_specs=[pl.BlockSpec((1, B_HEADS, nc, HEAD_PAD, qc), lambda bi, qi, kj: (bi, 0, qi, 0, 0)),
                  pl.BlockSpec((1, B_HEADS, tk, HEAD_PAD), lambda bi, qi, kj: (bi, 0, kj, 0)),
                  pl.BlockSpec((1, B_HEADS, B_V_DIM, tk), lambda bi, qi, kj: (bi, 0, 0, kj))],
        out_specs=pl.BlockSpec((1, tq, B_HEADS * B_V_DIM), lambda bi, qi, kj: (bi, qi, 0)),
        out_shape=jax.ShapeDtypeStruct((b, s_len, B_HEADS * B_V_DIM), BF16),
        scratch_shapes=[pltpu.VMEM((B_HEADS, nc, 1, qc), F32), pltpu.VMEM((B_HEADS, nc, 1, qc), F32),
                        pltpu.VMEM((B_HEADS, nc, B_V_DIM, qc), F32)]
                       + two((tk, qc), F32) + two((1, qc), F32) + two((tk, qc), BF16) + two((1, qc), F32),
        compiler_params=_cparams(("arbitrary", "arbitrary", "arbitrary")),
        name="mla",
    )(qbt, kb, vbt)


def _pack_pairs(lo, hi):
    lo_w = lax.bitcast_convert_type(lo.astype(BF16).astype(F32), jnp.uint32) >> 16
    hi_w = lax.bitcast_convert_type(hi.astype(BF16).astype(F32), jnp.uint32) & jnp.uint32(0xFFFF0000)
    return lo_w | hi_w


def _store_token_tiles(ref, v):
    n, d = v.shape
    words = _pack_pairs(v[:, :d // 2], v[:, d // 2:])
    for s in range(8):
        ref[pl.ds(s, n, stride=8), :] = words[:, 128 * s:128 * (s + 1)]


def _load_token_tiles(ref, n):
    lo, hi = [], []
    for s in range(8):
        w = ref[pl.ds(s, n, stride=8), :]
        lo.append(lax.bitcast_convert_type(w << 16, F32))
        hi.append(lax.bitcast_convert_type(w & jnp.uint32(0xFFFF0000), F32))
    return jnp.concatenate(lo, axis=1), jnp.concatenate(hi, axis=1)


def _argmax_rows(v, iota, n):
    m = jnp.max(v, axis=0, keepdims=True)
    i = jnp.min(jnp.where(v == m, iota, n), axis=0, keepdims=True)
    return m, i


def _route_tile(logits, bias_col):
    tm = logits.shape[0]
    gsz = N_EXPERTS // N_GROUPS
    s = jax.nn.sigmoid(logits.T[0:N_EXPERTS])
    sel = s + bias_col
    ninf = -jnp.inf
    g3 = sel.reshape(N_GROUPS, gsz, tm)
    w_iota = lax.broadcasted_iota(jnp.int32, g3.shape, 1).astype(F32)
    m1 = jnp.max(g3, axis=1, keepdims=True)
    i1 = jnp.min(jnp.where(g3 == m1, w_iota, float(gsz)), axis=1, keepdims=True)
    m2 = jnp.max(jnp.where(w_iota == i1, ninf, g3), axis=1, keepdims=True)
    gs = (m1 + m2).reshape(N_GROUPS, tm)
    g_iota = lax.broadcasted_iota(jnp.int32, gs.shape, 0).astype(F32)
    keep = jnp.zeros(gs.shape, F32)
    cur = gs
    for _ in range(TOPK_GROUPS):
        _, gi = _argmax_rows(cur, g_iota, float(N_GROUPS))
        hit = g_iota == gi
        keep = jnp.where(hit, 1.0, keep)
        cur = jnp.where(hit, ninf, cur)
    cur = jnp.where(keep.reshape(N_GROUPS, 1, tm) > 0.5, g3, ninf).reshape(N_EXPERTS, tm)
    e_iota = lax.broadcasted_iota(jnp.int32, cur.shape, 0).astype(F32)
    idx, wts = [], []
    hits = jnp.zeros(cur.shape, F32)
    for _ in range(TOP_K):
        _, ei = _argmax_rows(cur, e_iota, float(N_EXPERTS))
        hit = e_iota == ei
        idx.append(ei)
        wts.append(jnp.sum(jnp.where(hit, s, 0.0), axis=0, keepdims=True))
        hits = jnp.where(hit, 1.0, hits)
        cur = jnp.where(hit, ninf, cur)
    idx = jnp.concatenate(idx, axis=0)
    w = jnp.concatenate(wts, axis=0)
    w = w / jnp.sum(w, axis=0, keepdims=True) * ROUTED_SCALE
    return idx, w, hits


def _post_kernel(oa_ref, ob_ref, x_ref, mod_ref, g_ref, woa_ref, wob_ref, wr_ref, bias_ref, wsg_ref, wsu_ref,
                 wsd_ref, h2p_ref, x2_ref, idx_ref, rank_ref, w_ref, cnt_ref, run_ref):
    first = (pl.program_id(0) == 0) & (pl.program_id(1) == 0)

    @pl.when(first)
    def _():
        run_ref[...] = jnp.zeros(run_ref.shape, F32)

    att = (jnp.dot(oa_ref[0], woa_ref[...], preferred_element_type=F32)
           + jnp.dot(ob_ref[0], wob_ref[...], preferred_element_type=F32))
    g_a = mod_ref[0, 0:1, :]
    shift = mod_ref[0, 1:2, :]
    scale = mod_ref[0, 2:3, :]
    g_f = mod_ref[0, 3:4, :]
    xn = x_ref[0] + g_a * att
    h2 = _rms(xn, g_ref[...]) * (1.0 + scale) + shift
    h2b = h2.astype(BF16)
    _store_token_tiles(h2p_ref, h2)
    tm = h2.shape[0]

    h2_lo = (h2 - h2b.astype(F32)).astype(BF16)
    logits = (jnp.dot(h2b, wr_ref[0], preferred_element_type=F32)
              + jnp.dot(h2_lo, wr_ref[0], preferred_element_type=F32)
              + jnp.dot(h2b, wr_ref[1], preferred_element_type=F32))
    idx, w, hits = _route_tile(logits, bias_ref[...])
    idx_ref[...] = idx.astype(jnp.int32)
    w_ref[...] = jnp.concatenate([w, jnp.zeros((128 - TOP_K, tm), F32)], axis=0).T
    r_i = lax.broadcasted_iota(jnp.int32, (tm, tm), 0)
    c_i = lax.broadcasted_iota(jnp.int32, (tm, tm), 1)
    before = jnp.where(r_i < c_i, 1.0, 0.0).astype(BF16)
    hb = hits.astype(BF16)
    rank = jnp.dot(hb, before, preferred_element_type=F32) + run_ref[:, 0:1]
    e_iota = lax.broadcasted_iota(jnp.int32, rank.shape, 0).astype(F32)
    rows = [jnp.sum(jnp.where(e_iota == idx[k:k + 1], rank, 0.0), axis=0, keepdims=True) for k in range(TOP_K)]
    rank_ref[...] = jnp.concatenate(rows, axis=0).astype(jnp.int32)
    run_ref[...] += jnp.dot(hb, jnp.ones((tm, 128), BF16), preferred_element_type=F32)
    cnt_ref[...] = run_ref[...]

    gate = jnp.dot(h2b, wsg_ref[...], preferred_element_type=F32)
    up = jnp.dot(h2b, wsu_ref[...], preferred_element_type=F32)
    act = (gate * jax.nn.sigmoid(gate) * up).astype(BF16)
    shared = jnp.dot(act, wsd_ref[...], preferred_element_type=F32)
    x2_ref[0] = xn + g_f * shared


def _post(oa, ob, x, mod4, g_ffn, woa, wob, wr, bias_col, wsg, wsu, wsd):
    b, s_len, d = x.shape
    tm = ROW_TILE
    nt = s_len // tm
    t_tok = b * s_len
    row = lambda w: pl.BlockSpec((1, tm, w), lambda bi, i: (bi, i, 0))
    tok_t = pl.BlockSpec((TOP_K, tm), lambda bi, i: (0, bi * nt + i))
    return pl.pallas_call(
        _post_kernel,
        grid=(b, nt),
        in_specs=[row(1024), row(1024), row(d),
                  pl.BlockSpec((1, 4, d), lambda bi, i: (bi, 0, 0)),
                  _const_spec((1, d)), _const_spec(woa.shape), _const_spec(wob.shape), _const_spec(wr.shape),
                  _const_spec(bias_col.shape),
                  _const_spec(wsg.shape), _const_spec(wsu.shape), _const_spec(wsd.shape)],
        out_specs=[pl.BlockSpec((tm * 8, 128), lambda bi, i: (bi * nt + i, 0)),
                   row(d), tok_t, tok_t,
                   pl.BlockSpec((tm, 128), lambda bi, i: (bi * nt + i, 0)),
                   pl.BlockSpec((N_EXPERTS, 128), lambda bi, i: (0, 0))],
        out_shape=[jax.ShapeDtypeStruct((t_tok * 8, 128), jnp.uint32),
                   jax.ShapeDtypeStruct((b, s_len, d), F32),
                   jax.ShapeDtypeStruct((TOP_K, t_tok), jnp.int32),
                   jax.ShapeDtypeStruct((TOP_K, t_tok), jnp.int32),
                   jax.ShapeDtypeStruct((t_tok, 128), F32),
                   jax.ShapeDtypeStruct((N_EXPERTS, 128), F32)],
        scratch_shapes=[pltpu.VMEM((N_EXPERTS, 128), F32)],
        compiler_params=_cparams(("arbitrary", "arbitrary")),
        name="post",
    )(oa, ob, x, mod4, g_ffn, woa, wob, wr, bias_col, wsg, wsu, wsd)


def _plan_kernel(ps_ref, idx_ref, rank_ref, pos_ref):
    idx = idx_ref[...]
    base = jnp.zeros(idx.shape, jnp.int32)
    for e in range(N_EXPERTS):
        base = jnp.where(idx == e, ps_ref[e], base)
    pos_ref[...] = base + rank_ref[...]


def _plan(pad_start, idx_t, rank_t):
    k, t_tok = idx_t.shape
    tn = min(2048, t_tok)
    blk = lambda: pl.BlockSpec((k, tn), lambda j, ps: (0, j))
    return pl.pallas_call(
        _plan_kernel,
        grid_spec=pltpu.PrefetchScalarGridSpec(num_scalar_prefetch=1, grid=(t_tok // tn,),
                                               in_specs=[blk(), blk()], out_specs=blk()),
        out_shape=jax.ShapeDtypeStruct((k, t_tok), jnp.int32),
        compiler_params=_cparams(("arbitrary",)),
        name="plan",
    )(pad_start, idx_t, rank_t)


def _dispatch_kernel(lo_ref, hi_ref, h2p_ref, pos_hbm, xs_hbm, pos_smem, zero_ref, sem_pos, sem_rows, *, tm):
    i = pl.program_id(0)
    pos_cp = pltpu.make_async_copy(pos_hbm.at[:, pl.ds(i * tm, tm)], pos_smem, sem_pos)
    pos_cp.start()

    @pl.when(i == 0)
    def _():
        zero_ref[...] = jnp.zeros(zero_ref.shape, jnp.uint32)
        zrows = zero_ref.shape[0]

        def zero_cp(e):
            start = pl.multiple_of(hi_ref[e] * 8 - zrows, 8)
            return pltpu.make_async_copy(zero_ref, xs_hbm.at[pl.ds(start, zrows), :], sem_rows)

        def start_one(e, c):
            @pl.when(hi_ref[e] > lo_ref[e])
            def _():
                zero_cp(e).start()
            return c

        def wait_one(e, c):
            @pl.when(hi_ref[e] > lo_ref[e])
            def _():
                zero_cp(e).wait()
            return c
        lax.fori_loop(0, N_EXPERTS, start_one, 0)
        lax.fori_loop(0, N_EXPERTS, wait_one, 0)

    pos_cp.wait()

    def per_token(r, carry):
        src = h2p_ref.at[pl.ds(pl.multiple_of(r * 8, 8), 8), :]
        for k in range(TOP_K):
            p = pos_smem[k, r]
            pltpu.make_async_copy(src, xs_hbm.at[pl.ds(pl.multiple_of(p * 8, 8), 8), :],
                                  sem_rows).start(priority=k % 2)
        return carry
    lax.fori_loop(0, tm, per_token, 0)
    for k in range(TOP_K):
        pltpu.make_async_copy(h2p_ref, xs_hbm.at[pl.ds(0, tm * 8), :], sem_rows).wait()


def _dispatch(pad_lo, pad_hi, h2p, pos_t, p_len):
    t_tok = pos_t.shape[1]
    tm = ROW_TILE
    return pl.pallas_call(
        functools.partial(_dispatch_kernel, tm=tm),
        grid_spec=pltpu.PrefetchScalarGridSpec(
            num_scalar_prefetch=2, grid=(t_tok // tm,),
            in_specs=[pl.BlockSpec((tm * 8, 128), lambda i, lo, hi: (i, 0)),
                      pl.BlockSpec(memory_space=pl.ANY)],
            out_specs=pl.BlockSpec(memory_space=pl.ANY),
            scratch_shapes=[pltpu.SMEM((TOP_K, tm), jnp.int32), pltpu.VMEM((EXPERT_TILE * 8, 128), jnp.uint32),
                            pltpu.SemaphoreType.DMA, pltpu.SemaphoreType.DMA]),
        out_shape=jax.ShapeDtypeStruct((p_len * 8, 128), jnp.uint32),
        compiler_params=_cparams(("arbitrary",)),
        name="dispatch",
    )(pad_lo, pad_hi, h2p, pos_t)


def _experts_kernel(te_ref, nu_ref, x_ref, wg_ref, wu_ref, wd_ref, y_ref):
    t = pl.program_id(0)

    @pl.when(t < nu_ref[0])
    def _():
        tm = x_ref.shape[0] // 8
        lo, hi = _load_token_tiles(x_ref, tm)
        xb = jnp.concatenate([lo, hi], axis=1).astype(BF16)
        gate = jnp.dot(xb, wg_ref[0], preferred_element_type=F32)
        up = jnp.dot(xb, wu_ref[0], preferred_element_type=F32)
        act = (gate * jax.nn.sigmoid(gate) * up).astype(BF16)
        _store_token_tiles(y_ref, jnp.dot(act, wd_ref[0], preferred_element_type=F32))


def _experts(tile_expert, n_used, xs, wg, wu, wd):
    p_len = xs.shape[0] // 8
    d, ff = wg.shape[1], wg.shape[2]
    tm = EXPERT_TILE
    n_tiles = p_len // tm
    rows = pl.BlockSpec((tm * 8, 128), lambda t, te, nu: (jnp.minimum(t, nu[0] - 1), 0))
    grid_spec = pltpu.PrefetchScalarGridSpec(
        num_scalar_prefetch=2,
        grid=(n_tiles,),
        in_specs=[rows,
                  pl.BlockSpec((1, d, ff), lambda t, te, nu: (te[t], 0, 0)),
                  pl.BlockSpec((1, d, ff), lambda t, te, nu: (te[t], 0, 0)),
                  pl.BlockSpec((1, ff, d), lambda t, te, nu: (te[t], 0, 0))],
        out_specs=rows,
    )
    return pl.pallas_call(
        _experts_kernel,
        grid_spec=grid_spec,
        out_shape=jax.ShapeDtypeStruct((p_len * 8, 128), jnp.uint32),
        compiler_params=_cparams(("arbitrary",)),
        name="experts",
    )(tile_expert, n_used, xs, wg, wu, wd)


def _final_kernel(x2_ref, w_ref, gf_ref, g_ref, pos_hbm, ys_hbm, o_ref, pos_smem, buf_ref, sem_pos, sem_rows,
                  *, tm, nsteps):
    i = pl.program_id(0)
    slot = i % 2

    def pos_copy(step, s):
        return pltpu.make_async_copy(pos_hbm.at[:, pl.ds(step * tm, tm)], pos_smem.at[s], sem_pos.at[s])

    def issue_gathers(s):
        def per_token(r, carry):
            for k in range(TOP_K):
                p = pos_smem[s, k, r]
                pltpu.make_async_copy(ys_hbm.at[pl.ds(pl.multiple_of(p * 8, 8), 8), :],
                                      buf_ref.at[s, k, pl.ds(pl.multiple_of(r * 8, 8), 8), :],
                                      sem_rows.at[s]).start(priority=k % 2)
            return carry
        lax.fori_loop(0, tm, per_token, 0)

    @pl.when(i == 0)
    def _():
        pos_copy(0, 0).start()
        pos_copy(0, 0).wait()
        issue_gathers(0)
        if nsteps > 1:
            pos_copy(1, 1).start()

    @pl.when(i + 1 < nsteps)
    def _():
        pos_copy(i + 1, 1 - slot).wait()
        issue_gathers(1 - slot)

    @pl.when(i + 2 < nsteps)
    def _():
        pos_copy(i + 2, slot).start()

    for k in range(TOP_K):
        pltpu.make_async_copy(ys_hbm.at[pl.ds(0, tm * 8), :], buf_ref.at[slot, k], sem_rows.at[slot]).wait()

    w = w_ref[...]
    d = x2_ref.shape[1]
    lo_acc = jnp.zeros((tm, d // 2), F32)
    hi_acc = jnp.zeros((tm, d // 2), F32)
    for k in range(TOP_K):
        lo, hi = _load_token_tiles(buf_ref.at[slot, k], tm)
        lo_acc = lo_acc + w[:, k:k + 1] * lo
        hi_acc = hi_acc + w[:, k:k + 1] * hi
    routed = jnp.concatenate([lo_acc, hi_acc], axis=1)
    x = x2_ref[...] + gf_ref[0] * routed
    o_ref[...] = _rms(x, g_ref[...])


def _final(x2, w128, g_f, g_final, pos_t, ys):
    b, s_len, d = x2.shape
    t_tok = b * s_len
    tm = 128
    nt = s_len // tm
    nsteps = t_tok // tm
    row = pl.BlockSpec((tm, d), lambda i: (i, 0))
    out = pl.pallas_call(
        functools.partial(_final_kernel, tm=tm, nsteps=nsteps),
        grid=(nsteps,),
        in_specs=[row, pl.BlockSpec((tm, 128), lambda i: (i, 0)),
                  pl.BlockSpec((1, 1, d), lambda i: (i // nt, 0, 0)), _const_spec((1, d)),
                  pl.BlockSpec(memory_space=pl.ANY), pl.BlockSpec(memory_space=pl.ANY)],
        out_specs=row,
        out_shape=jax.ShapeDtypeStruct((t_tok, d), F32),
        scratch_shapes=[pltpu.SMEM((2, TOP_K, tm), jnp.int32), pltpu.VMEM((2, TOP_K, tm * 8, 128), jnp.uint32),
                        pltpu.SemaphoreType.DMA((2,)), pltpu.SemaphoreType.DMA((2,))],
        compiler_params=_cparams(("arbitrary",)),
        name="final",
    )(x2.reshape(t_tok, d), w128, g_f, g_final, pos_t, ys)
    return out.reshape(b, s_len, d)


def _rope_tables(s_len, c_len):
    rows = s_len // GRID_W
    r = jnp.repeat(jnp.arange(rows, dtype=F32), GRID_W)
    cidx = jnp.tile(jnp.arange(GRID_W, dtype=F32), rows)
    n_freq = 16
    inv = ROPE_BASE ** (-jnp.arange(n_freq, dtype=F32) / n_freq)
    ang = jnp.concatenate([r[:, None] * inv, cidx[:, None] * inv], axis=-1)
    cos = jnp.concatenate([jnp.cos(ang), jnp.ones((c_len, 32), F32)], axis=0)
    sin = jnp.concatenate([jnp.sin(ang), jnp.zeros((c_len, 32), F32)], axis=0)
    rope_a = jnp.concatenate([jnp.tile(cos, (1, 4)), jnp.tile(sin, (1, 4))], axis=1)
    rope_k = jnp.concatenate([cos, cos, sin, sin], axis=1)
    rope_t = jnp.concatenate([cos.T, sin.T], axis=0)
    return rope_a, rope_k, rope_t


def _in_weight(w_in):
    qa_cols = np.empty((4, 2, 4, 32), np.int32)
    for r in range(4):
        for p in range(2):
            for j in range(4):
                qa_cols[r, p, j] = (4 * j + r) * 64 + p * 32 + np.arange(32)
    ka_cols = np.empty((2, 4, 32), np.int32)
    for p in range(2):
        for j in range(4):
            ka_cols[p, j] = 1024 + j * 64 + p * 32 + np.arange(32)
    cols = np.concatenate([qa_cols.reshape(-1), ka_cols.reshape(-1), np.arange(1280, 2368)])
    w = w_in[:, cols]
    kr = w_in[:, 2304:2368]
    kr_rot = jnp.concatenate([-kr[:, 32:], kr[:, :32]], axis=1)
    return jnp.concatenate([w, kr_rot], axis=1).astype(BF16)


def kernel(x, c, ctx, c_ctx, w_mod, b_mod, norm_attn_g, norm_ffn_g, w_in, attn_sink, q_a_norm_g, w_uq,
           kv_a_norm_g, w_ukv, w_out, w_router, router_bias, w_gate, w_up, w_down, ws_gate, ws_up, ws_down,
           norm_final_g):
    b, s_len, d = x.shape
    c_len = ctx.shape[1]
    l_total = s_len + c_len
    t_tok = b * s_len

    c_rows = jnp.zeros((8, d), F32).at[:b].set(c).at[b].set(c_ctx)
    mod = _mod(c_rows, w_mod[0], b_mod[0][None, :])
    mod6 = mod.reshape(8, 6, d)
    sh_a, sc_a, g_a, sh_f, sc_f, g_f = (mod6[:b, k] for k in range(6))
    ms_x = jnp.stack([sh_a, sc_a], axis=1)
    ms_c = jnp.broadcast_to(jnp.stack([mod6[b, 0], mod6[b, 1]], axis=0)[None], (b, 2, d))

    w_in_p = _in_weight(w_in[0])
    w_uq_h = w_uq[0].reshape(Q_LORA_RANK, B_HEADS, B_QK_DIM)
    w_uqt = jnp.pad(jnp.transpose(w_uq_h, (1, 2, 0)), ((0, 0), (0, HEAD_PAD - B_QK_DIM), (0, 0)))
    w_uqt = w_uqt.reshape(B_HEADS * HEAD_PAD, Q_LORA_RANK).astype(BF16)
    w_ukv_h = w_ukv[0].reshape(KV_LORA_RANK, B_HEADS, B_NOPE_DIM + B_V_DIM)
    w_uk = w_ukv_h[:, :, :B_NOPE_DIM].reshape(KV_LORA_RANK, B_HEADS * B_NOPE_DIM).astype(BF16)
    w_uvt = w_ukv_h[:, :, B_NOPE_DIM:].reshape(KV_LORA_RANK, B_HEADS * B_V_DIM).T.astype(BF16)
    rope_a, rope_k, rope_t = _rope_tables(s_len, c_len)

    g_attn = norm_attn_g[0][None, :]
    gq = q_a_norm_g[0][None, :]
    gkv = kv_a_norm_g[0][None, :]
    proj_w = (g_attn, w_in_p, gq, w_uqt, gkv, w_uk, w_uvt, rope_a, rope_k, rope_t)
    outs = _inproj(x, ms_x, *proj_w, l_total, 0)
    qa, ka, va, qbt, kb, vbt = _inproj(ctx, ms_c, *proj_w, l_total, s_len // ROW_TILE, prev=outs)

    stack_heads = np.array([4 * (s % 4) + s // 4 for s in range(16)], np.int32)
    sink_rows = jnp.repeat(attn_sink[0][stack_heads] * LOG2E, SWA_BLOCK)[:, None].astype(F32)
    o_a = _swa(qa, ka, va, sink_rows, s_len, c_len)
    o_b = _mla(qbt, kb, vbt, s_len)

    oa_rows = np.empty((4, 4, 64), np.int32)
    for r in range(4):
        for j in range(4):
            oa_rows[r, j] = (4 * j + r) * 64 + np.arange(64)
    woa = w_out[0][oa_rows.reshape(-1)].astype(BF16)
    wob = w_out[0][1024:].astype(BF16)
    wr = jnp.pad(w_router[0], ((0, 0), (0, 128 - N_EXPERTS)))
    wr_hi = wr.astype(BF16)
    wr = jnp.stack([wr_hi, (wr - wr_hi.astype(F32)).astype(BF16)], axis=0)
    mod4 = jnp.stack([g_a, sh_f, sc_f, g_f], axis=1)
    h2p, x2, idx_t, rank_t, w128, cnt = _post(
        o_a, o_b, x, mod4, norm_ffn_g[0][None, :], woa, wob, wr, router_bias[0][:, None].astype(F32),
        ws_gate[0].astype(BF16), ws_up[0].astype(BF16), ws_down[0].astype(BF16))

    tm = EXPERT_TILE
    counts = cnt[:, 0].astype(jnp.int32)
    padded = (counts + tm - 1) // tm * tm
    pad_end = jnp.cumsum(padded).astype(jnp.int32)
    pad_start = pad_end - padded
    n_tiles = -(-t_tok * TOP_K // tm) + N_EXPERTS
    tile_row0 = jnp.arange(n_tiles, dtype=jnp.int32) * tm
    tile_expert = jnp.minimum(jnp.sum((pad_end[None, :] <= tile_row0[:, None]).astype(jnp.int32), axis=1),
                              N_EXPERTS - 1)
    n_used = (pad_end[-1] // tm).astype(jnp.int32)[None]

    pos_t = _plan(pad_start, idx_t, rank_t)
    xs = _dispatch(pad_start, pad_end, h2p, pos_t, n_tiles * tm)
    ys = _experts(tile_expert, n_used, xs, w_gate[0].astype(BF16), w_up[0].astype(BF16), w_down[0].astype(BF16))
    return _final(x2, w128, g_f[:, None, :], norm_final_g[None, :], pos_t, ys)
```

```python
import functools
import math

import numpy as np
import jax
import jax.numpy as jnp
from jax import lax
from jax.experimental import pallas as pl
from jax.experimental.pallas import tpu as pltpu

F32 = jnp.float32
BF16 = jnp.bfloat16

GRID_W = 64
ROPE_BASE = 10000.0
EPS = 1e-6
A_HEADS = 16
A_KV_HEADS = 4
A_HEAD_DIM = 64
WINDOW = 128
B_HEADS = 8
B_NOPE_DIM = 128
B_ROPE_DIM = 64
B_QK_DIM = B_NOPE_DIM + B_ROPE_DIM
B_V_DIM = 128
Q_LORA_RANK = 512
KV_LORA_RANK = 256
N_EXPERTS = 64
TOP_K = 8
N_GROUPS = 8
TOPK_GROUPS = 4
ROUTED_SCALE = 2.5

LOG2E = math.log2(math.e)
NEG_BIG = -1e30

VMEM_LIMIT_BYTES = 56 * 1024 * 1024

ROW_TILE = 256
SWA_BLOCK = 128
MLA_TQ = 1024
MLA_TK = 768
MLA_KC = 128
EXPERT_TILE = 256
HEAD_PAD = 256

IN_COLS_PADDED = 2432


def _cparams(sem):
    return pltpu.CompilerParams(dimension_semantics=sem, vmem_limit_bytes=VMEM_LIMIT_BYTES)


def _const_spec(shape):
    n = len(shape)
    return pl.BlockSpec(shape, lambda *_: (0,) * n, pipeline_mode=pl.Buffered(1))


def _mod_kernel(c_ref, w_ref, b_ref, o_ref):
    c = c_ref[...]
    a = c * jax.nn.sigmoid(c)
    o_ref[...] = jnp.dot(a, w_ref[...], preferred_element_type=F32,
                         precision=lax.Precision.HIGHEST) + b_ref[...]


def _mod(c_rows, w_mod, b_mod):
    d, n = w_mod.shape
    tn = 1024
    return pl.pallas_call(
        _mod_kernel,
        grid=(n // tn,),
        in_specs=[pl.BlockSpec((8, d), lambda j: (0, 0)),
                  pl.BlockSpec((d, tn), lambda j: (0, j)),
                  pl.BlockSpec((1, tn), lambda j: (0, j))],
        out_specs=pl.BlockSpec((8, tn), lambda j: (0, j)),
        out_shape=jax.ShapeDtypeStruct((8, n), F32),
        compiler_params=_cparams(("arbitrary",)),
        name="mod",
    )(c_rows, w_mod, b_mod)


def _rms(x, g):
    var = jnp.mean(x * x, axis=-1, keepdims=True)
    return x * lax.rsqrt(var + EPS) * g


def _inproj_kernel(x_ref, ms_ref, g_ref, w_in_ref, gq_ref, w_uqt_ref, gkv_ref, w_uk_ref, w_uvt_ref,
                   rope_a_ref, rope_k_ref, rope_t_ref, *rest, sa, sb):
    qa_ref, ka_ref, va_ref, qbt_ref, kb_ref, vbt_ref = rest[-6:]
    x = x_ref[0]
    shift = ms_ref[0, 0:1, :]
    scale = ms_ref[0, 1:2, :]
    h = _rms(x, g_ref[...]) * (1.0 + scale) + shift
    z = jnp.dot(h.astype(BF16), w_in_ref[...], preferred_element_type=F32)

    cos_a = rope_a_ref[:, 0:128]
    sin_a = rope_a_ref[:, 128:256]
    for r in range(4):
        q1 = z[:, 256 * r:256 * r + 128]
        q2 = z[:, 256 * r + 128:256 * r + 256]
        qa_ref[0, 256 * r:256 * r + 128, :] = ((q1 * cos_a - q2 * sin_a) * sa).T.astype(BF16)
        qa_ref[0, 256 * r + 128:256 * r + 256, :] = ((q2 * cos_a + q1 * sin_a) * sa).T.astype(BF16)
    k1 = z[:, 1024:1152]
    k2 = z[:, 1152:1280]
    ka_ref[0, :, 0:128] = (k1 * cos_a - k2 * sin_a).astype(BF16)
    ka_ref[0, :, 128:256] = (k2 * cos_a + k1 * sin_a).astype(BF16)
    va_ref[0] = z[:, 1280:1536].T.astype(BF16)

    cqn = _rms(z[:, 1536:2048], gq_ref[...])
    qt = jnp.dot(w_uqt_ref[...], cqn.T.astype(BF16), preferred_element_type=F32)
    cos_t = rope_t_ref[0:32, :]
    sin_t = rope_t_ref[32:64, :]
    for hh in range(B_HEADS):
        base = HEAD_PAD * hh
        x1 = qt[base + 128:base + 160]
        x2 = qt[base + 160:base + 192]
        qbt_ref[0, hh, 0, 0:128, :] = (qt[base:base + 128] * sb).astype(BF16)
        qbt_ref[0, hh, 0, 128:160, :] = ((x1 * cos_t - x2 * sin_t) * sb).astype(BF16)
        qbt_ref[0, hh, 0, 160:192, :] = ((x2 * cos_t + x1 * sin_t) * sb).astype(BF16)
        qbt_ref[0, hh, 0, 192:256, :] = jnp.zeros((64, x.shape[0]), BF16)

    ckvn = _rms(z[:, 2048:2304], gkv_ref[...])
    knope = jnp.dot(ckvn.astype(BF16), w_uk_ref[...], preferred_element_type=F32)
    vt = jnp.dot(w_uvt_ref[...], ckvn.T.astype(BF16), preferred_element_type=F32)
    t = z[:, 2304:2432] * rope_k_ref[...]
    lane = lax.broadcasted_iota(jnp.int32, t.shape, 1)
    krp = jnp.where(lane < 64, t + pltpu.roll(t, 64, 1), 0.0).astype(BF16)
    for hh in range(B_HEADS):
        vbt_ref[0, hh] = vt[B_V_DIM * hh:B_V_DIM * (hh + 1)].astype(BF16)
        kb_ref[0, hh, :, 0:128] = knope[:, 128 * hh:128 * hh + 128].astype(BF16)
        kb_ref[0, hh, :, 128:256] = krp


def _inproj(x, ms, g_attn, w_in_p, gq, w_uqt, gkv, w_uk, w_uvt, rope_a, rope_k, rope_t,
            l_total, row_block0, prev=None):
    b, n, d = x.shape
    tm = ROW_TILE
    nt = n // tm
    sa = (A_HEAD_DIM ** -0.5) * LOG2E
    sb = (B_QK_DIM ** -0.5) * LOG2E
    out_shape = [
        jax.ShapeDtypeStruct((b, A_HEADS * A_HEAD_DIM, l_total), BF16),
        jax.ShapeDtypeStruct((b, l_total, A_KV_HEADS * A_HEAD_DIM), BF16),
        jax.ShapeDtypeStruct((b, A_KV_HEADS * A_HEAD_DIM, l_total), BF16),
        jax.ShapeDtypeStruct((b, B_HEADS, l_total // tm, HEAD_PAD, tm), BF16),
        jax.ShapeDtypeStruct((b, B_HEADS, l_total, HEAD_PAD), BF16),
        jax.ShapeDtypeStruct((b, B_HEADS, B_V_DIM, l_total), BF16),
    ]
    r0 = row_block0
    out_specs = [
        pl.BlockSpec((1, 1024, tm), lambda bi, i: (bi, 0, r0 + i)),
        pl.BlockSpec((1, tm, 256), lambda bi, i: (bi, r0 + i, 0)),
        pl.BlockSpec((1, 256, tm), lambda bi, i: (bi, 0, r0 + i)),
        pl.BlockSpec((1, B_HEADS, 1, HEAD_PAD, tm), lambda bi, i: (bi, 0, r0 + i, 0, 0)),
        pl.BlockSpec((1, B_HEADS, tm, HEAD_PAD), lambda bi, i: (bi, 0, r0 + i, 0)),
        pl.BlockSpec((1, B_HEADS, B_V_DIM, tm), lambda bi, i: (bi, 0, 0, r0 + i)),
    ]
    in_specs = [
        pl.BlockSpec((1, tm, d), lambda bi, i: (bi, i, 0)),
        pl.BlockSpec((1, 2, d), lambda bi, i: (bi, 0, 0)),
        _const_spec((1, d)),
        _const_spec(w_in_p.shape),
        _const_spec((1, Q_LORA_RANK)),
        _const_spec(w_uqt.shape),
        _const_spec((1, KV_LORA_RANK)),
        _const_spec(w_uk.shape),
        _const_spec(w_uvt.shape),
        pl.BlockSpec((tm, 256), lambda bi, i: (r0 + i, 0)),
        pl.BlockSpec((tm, 128), lambda bi, i: (r0 + i, 0)),
        pl.BlockSpec((64, tm), lambda bi, i: (0, r0 + i)),
    ]
    args = [x, ms, g_attn, w_in_p, gq, w_uqt, gkv, w_uk, w_uvt, rope_a, rope_k, rope_t]
    aliases = {}
    if prev is not None:
        for k, p in enumerate(prev):
            in_specs.append(pl.BlockSpec(memory_space=pl.ANY))
            aliases[len(args)] = k
            args.append(p)
    return pl.pallas_call(
        functools.partial(_inproj_kernel, sa=sa, sb=sb),
        grid=(b, nt),
        in_specs=in_specs,
        out_specs=out_specs,
        out_shape=out_shape,
        input_output_aliases=aliases,
        compiler_params=_cparams(("arbitrary", "arbitrary")),
        name="inproj_ctx" if prev is not None else "inproj",
    )(*args)


def _swa_kernel(qt_ref, kp_ref, kc_ref, kn_ref, vp_ref, vc_ref, vn_ref, kx_ref, vx_ref, sink_ref, o_ref,
                s0, s1, bm0, bm1, p0, p1, d0, d1, *, nblk):
    i = pl.program_id(1)
    s_refs, bm_refs, p_refs, d_refs = (s0, s1), (bm0, bm1), (p0, p1), (d0, d1)
    nwin = 3 * SWA_BLOCK
    kall = jnp.concatenate([kp_ref[0], kc_ref[0], kn_ref[0], kx_ref[0]], axis=0)
    vall = jnp.concatenate([vp_ref[0], vc_ref[0], vn_ref[0], vx_ref[0]], axis=1)
    ck = lax.broadcasted_iota(jnp.int32, (nwin, SWA_BLOCK), 0)
    rq = lax.broadcasted_iota(jnp.int32, (nwin, SWA_BLOCK), 1)
    band = jnp.abs(ck - WINDOW - rq) <= WINDOW
    band = band & ((ck >= SWA_BLOCK) | (i > 0)) & ((ck < 2 * SWA_BLOCK) | (i < nblk - 1))
    band4 = jnp.concatenate([band] * A_KV_HEADS, axis=1)

    def scores(c):
        q1 = qt_ref[0, 256 * c:256 * c + 128, :]
        q2 = qt_ref[0, 256 * c + 128:256 * c + 256, :]
        cols = []
        for j in range(A_KV_HEADS):
            rows = slice(32 * j, 32 * j + 32)
            pieces = []
            for half in (q1, q2):
                if j > 0:
                    pieces.append(jnp.zeros((32 * j, SWA_BLOCK), BF16))
                pieces.append(half[rows])
                if j < A_KV_HEADS - 1:
                    pieces.append(jnp.zeros((96 - 32 * j, SWA_BLOCK), BF16))
            cols.append(jnp.concatenate(pieces, axis=0))
        st = jnp.dot(kall, jnp.concatenate(cols, axis=1), preferred_element_type=F32)
        st = jnp.concatenate([jnp.where(band4, st[:nwin], NEG_BIG), st[nwin:]], axis=0)
        s_refs[c % 2][...] = st
        bm_refs[c % 2][...] = jnp.max(st, axis=0, keepdims=True)

    def softmax(c):
        sink = sink_ref[:, 512 * c:512 * (c + 1)]
        m = jnp.maximum(bm_refs[c % 2][...], sink)
        p = jnp.exp2(s_refs[c % 2][...] - m)
        d_refs[c % 2][...] = jnp.sum(p, axis=0, keepdims=True) + jnp.exp2(sink - m)
        p_refs[c % 2][...] = p.astype(BF16)

    def values(c):
        ot = jnp.dot(vall, p_refs[c % 2][...], preferred_element_type=F32)
        ot = ot * (1.0 / d_refs[c % 2][...])
        oc = jnp.concatenate([ot[64 * j:64 * j + 64, SWA_BLOCK * j:SWA_BLOCK * (j + 1)]
                              for j in range(A_KV_HEADS)], axis=0)
        o_ref[0, :, 256 * c:256 * c + 256] = oc.T.astype(BF16)

    n_it = A_HEADS // A_KV_HEADS
    scores(0)
    for c in range(n_it):
        if c + 1 < n_it:
            scores(c + 1)
        if c >= 1:
            values(c - 1)
        softmax(c)
    values(n_it - 1)


def _swa(qat, ka, vat, sink_row, s_len, c_len):
    b = qat.shape[0]
    nblk = s_len // SWA_BLOCK
    nkeys = 3 * SWA_BLOCK + c_len
    krow = lambda f: pl.BlockSpec((1, SWA_BLOCK, 256), f)
    vcol = lambda f: pl.BlockSpec((1, 256, SWA_BLOCK), f)
    prev_r = lambda bi, i: (bi, jnp.maximum(i - 1, 0), 0)
    cur_r = lambda bi, i: (bi, i, 0)
    next_r = lambda bi, i: (bi, jnp.minimum(i + 1, nblk - 1), 0)
    prev_c = lambda bi, i: (bi, 0, jnp.maximum(i - 1, 0))
    cur_c = lambda bi, i: (bi, 0, i)
    next_c = lambda bi, i: (bi, 0, jnp.minimum(i + 1, nblk - 1))
    two = lambda shape, dt: [pltpu.VMEM(shape, dt), pltpu.VMEM(shape, dt)]
    return pl.pallas_call(
        functools.partial(_swa_kernel, nblk=nblk),
        grid=(b, nblk),
        in_specs=[pl.BlockSpec((1, 1024, SWA_BLOCK), cur_c),
                  krow(prev_r), krow(cur_r), krow(next_r), vcol(prev_c), vcol(cur_c), vcol(next_c),
                  pl.BlockSpec((1, c_len, 256), lambda bi, i: (bi, s_len // c_len, 0)),
                  pl.BlockSpec((1, 256, c_len), lambda bi, i: (bi, 0, s_len // c_len)),
                  _const_spec(sink_row.shape)],
        out_specs=pl.BlockSpec((1, SWA_BLOCK, 1024), cur_r),
        out_shape=jax.ShapeDtypeStruct((b, s_len, 1024), BF16),
        scratch_shapes=two((nkeys, 512), F32) + two((1, 512), F32) + two((nkeys, 512), BF16) + two((1, 512), F32),
        compiler_params=_cparams(("arbitrary", "arbitrary")),
        name="swa",
    )(qat, ka, ka, ka, vat, vat, vat, ka, vat, sink_row)


def _mla_kernel(qt_ref, k_ref, vt_ref, o_ref, m_ref, l_ref, acc_ref, s0, s1, bm0, bm1, p0, p1, a0, a1, *, nk, nc):
    kj = pl.program_id(2)
    tk = k_ref.shape[2]
    qc = qt_ref.shape[4]
    n_it = B_HEADS * nc
    s_refs, bm_refs, p_refs, a_refs = (s0, s1), (bm0, bm1), (p0, p1), (a0, a1)

    @pl.when(kj == 0)
    def _():
        m_ref[...] = jnp.full(m_ref.shape, NEG_BIG, F32)
        l_ref[...] = jnp.zeros(l_ref.shape, F32)
        acc_ref[...] = jnp.zeros(acc_ref.shape, F32)

    def scores(i):
        h, c = divmod(i, nc)
        st = jnp.dot(k_ref[0, h], qt_ref[0, h, c], preferred_element_type=F32)
        s_refs[i % 2][...] = st
        bm_refs[i % 2][...] = jnp.max(st, axis=0, keepdims=True)

    def softmax(i):
        h, c = divmod(i, nc)
        m_old = m_ref[h, c]
        m_new = jnp.maximum(m_old, bm_refs[i % 2][...])
        alpha = jnp.exp2(m_old - m_new)
        m_ref[h, c] = m_new
        a_refs[i % 2][...] = alpha
        lsum = jnp.zeros((1, qc), F32)
        for r in range(tk // MLA_KC):
            rows = slice(MLA_KC * r, MLA_KC * (r + 1))
            p = jnp.exp2(s_refs[i % 2][rows, :] - m_new)
            lsum = lsum + jnp.sum(p, axis=0, keepdims=True)
            p_refs[i % 2][rows, :] = p.astype(BF16)
        l_ref[h, c] = alpha * l_ref[h, c] + lsum

    def values(i):
        h, c = divmod(i, nc)
        pv = jnp.dot(vt_ref[0, h], p_refs[i % 2][...], preferred_element_type=F32)
        acc_ref[h, c] = a_refs[i % 2][...] * acc_ref[h, c] + pv

    scores(0)
    for i in range(n_it):
        if i + 1 < n_it:
            scores(i + 1)
        if i >= 1:
            values(i - 1)
        softmax(i)
    values(n_it - 1)

    @pl.when(kj == nk - 1)
    def _():
        for hh in range(B_HEADS):
            for c in range(nc):
                o = acc_ref[hh, c] * (1.0 / l_ref[hh, c])
                o_ref[0, qc * c:qc * (c + 1), B_V_DIM * hh:B_V_DIM * (hh + 1)] = o.T.astype(BF16)


def _mla(qbt, kb, vbt, s_len):
    b, _, _, _, qc = qbt.shape
    l_total = kb.shape[2]
    tq = min(MLA_TQ, s_len)
    tk = MLA_TK if l_total % MLA_TK == 0 else 256
    nc = tq // qc
    nq = s_len // tq
    nk = l_total // tk
    two = lambda shape, dt: [pltpu.VMEM(shape, dt), pltpu.VMEM(shape, dt)]
    return pl.pallas_call(
        functools.partial(_mla_kernel, nk=nk, nc=nc),
        grid=(b, nq, nk),
        in_specs=[pl.BlockSpec((1, B_HEADS, nc, HEAD_PAD, qc), lambda bi, qi, kj: (bi, 0, qi, 0, 0)),
                  pl.BlockSpec((1, B_HEADS, tk, HEAD_PAD), lambda bi, qi, kj: (bi, 0, kj, 0)),
                  pl.BlockSpec((1, B_HEADS, B_V_DIM, tk), lambda bi, qi, kj: (bi, 0, 0, kj))],
        out_specs=pl.BlockSpec((1, tq, B_HEADS * B_V_DIM), lambda bi, qi, kj: (bi, qi, 0)),
        out_shape=jax.ShapeDtypeStruct((b, s_len, B_HEADS * B_V_DIM), BF16),
        scratch_shapes=[pltpu.VMEM((B_HEADS, nc, 1, qc), F32), pltpu.VMEM((B_HEADS, nc, 1, qc), F32),
                        pltpu.VMEM((B_HEADS, nc, B_V_DIM, qc), F32)]
                       + two((tk, qc), F32) + two((1, qc), F32) + two((tk, qc), BF16) + two((1, qc), F32),
        compiler_params=_cparams(("arbitrary", "arbitrary", "arbitrary")),
        name="mla",
    )(qbt, kb, vbt)


def _pack_pairs(lo, hi):
    lo_w = lax.bitcast_convert_type(lo.astype(BF16).astype(F32), jnp.uint32) >> 16
    hi_w = lax.bitcast_convert_type(hi.astype(BF16).astype(F32), jnp.uint32) & jnp.uint32(0xFFFF0000)
    return lo_w | hi_w


def _store_token_tiles(ref, v):
    n, d = v.shape
    words = _pack_pairs(v[:, :d // 2], v[:, d // 2:])
    for s in range(8):
        ref[pl.ds(s, n, stride=8), :] = words[:, 128 * s:128 * (s + 1)]


def _load_token_tiles(ref, n):
    lo, hi = [], []
    for s in range(8):
        w = ref[pl.ds(s, n, stride=8), :]
        lo.append(lax.bitcast_convert_type(w << 16, F32))
        hi.append(lax.bitcast_convert_type(w & jnp.uint32(0xFFFF0000), F32))
    return jnp.concatenate(lo, axis=1), jnp.concatenate(hi, axis=1)


def _argmax_rows(v, iota, n):
    m = jnp.max(v, axis=0, keepdims=True)
    i = jnp.min(jnp.where(v == m, iota, n), axis=0, keepdims=True)
    return m, i


def _route_tile(logits, bias_col):
    tm = logits.shape[0]
    gsz = N_EXPERTS // N_GROUPS
    s = jax.nn.sigmoid(logits.T[0:N_EXPERTS])
    sel = s + bias_col
    ninf = -jnp.inf
    g3 = sel.reshape(N_GROUPS, gsz, tm)
    w_iota = lax.broadcasted_iota(jnp.int32, g3.shape, 1).astype(F32)
    m1 = jnp.max(g3, axis=1, keepdims=True)
    i1 = jnp.min(jnp.where(g3 == m1, w_iota, float(gsz)), axis=1, keepdims=True)
    m2 = jnp.max(jnp.where(w_iota == i1, ninf, g3), axis=1, keepdims=True)
    gs = (m1 + m2).reshape(N_GROUPS, tm)
    g_iota = lax.broadcasted_iota(jnp.int32, gs.shape, 0).astype(F32)
    keep = jnp.zeros(gs.shape, F32)
    cur = gs
    for _ in range(TOPK_GROUPS):
        _, gi = _argmax_rows(cur, g_iota, float(N_GROUPS))
        hit = g_iota == gi
        keep = jnp.where(hit, 1.0, keep)
        cur = jnp.where(hit, ninf, cur)
    cur = jnp.where(keep.reshape(N_GROUPS, 1, tm) > 0.5, g3, ninf).reshape(N_EXPERTS, tm)
    e_iota = lax.broadcasted_iota(jnp.int32, cur.shape, 0).astype(F32)
    idx, wts = [], []
    hits = jnp.zeros(cur.shape, F32)
    for _ in range(TOP_K):
        _, ei = _argmax_rows(cur, e_iota, float(N_EXPERTS))
        hit = e_iota == ei
        idx.append(ei)
        wts.append(jnp.sum(jnp.where(hit, s, 0.0), axis=0, keepdims=True))
        hits = jnp.where(hit, 1.0, hits)
        cur = jnp.where(hit, ninf, cur)
    idx = jnp.concatenate(idx, axis=0)
    w = jnp.concatenate(wts, axis=0)
    w = w / jnp.sum(w, axis=0, keepdims=True) * ROUTED_SCALE
    return idx, w, hits


def _post_kernel(oa_ref, ob_ref, x_ref, mod_ref, g_ref, woa_ref, wob_ref, wr_ref, bias_ref, wsg_ref, wsu_ref,
                 wsd_ref, h2p_ref, x2_ref, idx_ref, rank_ref, w_ref, cnt_ref, run_ref):
    first = (pl.program_id(0) == 0) & (pl.program_id(1) == 0)

    @pl.when(first)
    def _():
        run_ref[...] = jnp.zeros(run_ref.shape, F32)

    att = (jnp.dot(oa_ref[0], woa_ref[...], preferred_element_type=F32)
           + jnp.dot(ob_ref[0], wob_ref[...], preferred_element_type=F32))
    g_a = mod_ref[0, 0:1, :]
    shift = mod_ref[0, 1:2, :]
    scale = mod_ref[0, 2:3, :]
    g_f = mod_ref[0, 3:4, :]
    xn = x_ref[0] + g_a * att
    h2 = _rms(xn, g_ref[...]) * (1.0 + scale) + shift
    h2b = h2.astype(BF16)
    _store_token_tiles(h2p_ref, h2)
    tm = h2.shape[0]

    h2_lo = (h2 - h2b.astype(F32)).astype(BF16)
    logits = (jnp.dot(h2b, wr_ref[0], preferred_element_type=F32)
              + jnp.dot(h2_lo, wr_ref[0], preferred_element_type=F32)
              + jnp.dot(h2b, wr_ref[1], preferred_element_type=F32))
    idx, w, hits = _route_tile(logits, bias_ref[...])
    idx_ref[...] = idx.astype(jnp.int32)
    w_ref[...] = jnp.concatenate([w, jnp.zeros((128 - TOP_K, tm), F32)], axis=0).T
    r_i = lax.broadcasted_iota(jnp.int32, (tm, tm), 0)
    c_i = lax.broadcasted_iota(jnp.int32, (tm, tm), 1)
    before = jnp.where(r_i < c_i, 1.0, 0.0).astype(BF16)
    hb = hits.astype(BF16)
    rank = jnp.dot(hb, before, preferred_element_type=F32) + run_ref[:, 0:1]
    e_iota = lax.broadcasted_iota(jnp.int32, rank.shape, 0).astype(F32)
    rows = [jnp.sum(jnp.where(e_iota == idx[k:k + 1], rank, 0.0), axis=0, keepdims=True) for k in range(TOP_K)]
    rank_ref[...] = jnp.concatenate(rows, axis=0).astype(jnp.int32)
    run_ref[...] += jnp.dot(hb, jnp.ones((tm, 128), BF16), preferred_element_type=F32)
    cnt_ref[...] = run_ref[...]

    gate = jnp.dot(h2b, wsg_ref[...], preferred_element_type=F32)
    up = jnp.dot(h2b, wsu_ref[...], preferred_element_type=F32)
    act = (gate * jax.nn.sigmoid(gate) * up).astype(BF16)
    shared = jnp.dot(act, wsd_ref[...], preferred_element_type=F32)
    x2_ref[0] = xn + g_f * shared


def _post(oa, ob, x, mod4, g_ffn, woa, wob, wr, bias_col, wsg, wsu, wsd):
    b, s_len, d = x.shape
    tm = ROW_TILE
    nt = s_len // tm
    t_tok = b * s_len
    row = lambda w: pl.BlockSpec((1, tm, w), lambda bi, i: (bi, i, 0))
    tok_t = pl.BlockSpec((TOP_K, tm), lambda bi, i: (0, bi * nt + i))
    return pl.pallas_call(
        _post_kernel,
        grid=(b, nt),
        in_specs=[row(1024), row(1024), row(d),
                  pl.BlockSpec((1, 4, d), lambda bi, i: (bi, 0, 0)),
                  _const_spec((1, d)), _const_spec(woa.shape), _const_spec(wob.shape), _const_spec(wr.shape),
                  _const_spec(bias_col.shape),
                  _const_spec(wsg.shape), _const_spec(wsu.shape), _const_spec(wsd.shape)],
        out_specs=[pl.BlockSpec((tm * 8, 128), lambda bi, i: (bi * nt + i, 0)),
                   row(d), tok_t, tok_t,
                   pl.BlockSpec((tm, 128), lambda bi, i: (bi * nt + i, 0)),
                   pl.BlockSpec((N_EXPERTS, 128), lambda bi, i: (0, 0))],
        out_shape=[jax.ShapeDtypeStruct((t_tok * 8, 128), jnp.uint32),
                   jax.ShapeDtypeStruct((b, s_len, d), F32),
                   jax.ShapeDtypeStruct((TOP_K, t_tok), jnp.int32),
                   jax.ShapeDtypeStruct((TOP_K, t_tok), jnp.int32),
                   jax.ShapeDtypeStruct((t_tok, 128), F32),
                   jax.ShapeDtypeStruct((N_EXPERTS, 128), F32)],
        scratch_shapes=[pltpu.VMEM((N_EXPERTS, 128), F32)],
        compiler_params=_cparams(("arbitrary", "arbitrary")),
        name="post",
    )(oa, ob, x, mod4, g_ffn, woa, wob, wr, bias_col, wsg, wsu, wsd)


def _plan_kernel(ps_ref, idx_ref, rank_ref, pos_ref):
    idx = idx_ref[...]
    base = jnp.zeros(idx.shape, jnp.int32)
    for e in range(N_EXPERTS):
        base = jnp.where(idx == e, ps_ref[e], base)
    pos_ref[...] = base + rank_ref[...]


def _plan(pad_start, idx_t, rank_t):
    k, t_tok = idx_t.shape
    tn = min(2048, t_tok)
    blk = lambda: pl.BlockSpec((k, tn), lambda j, ps: (0, j))
    return pl.pallas_call(
        _plan_kernel,
        grid_spec=pltpu.PrefetchScalarGridSpec(num_scalar_prefetch=1, grid=(t_tok // tn,),
                                               in_specs=[blk(), blk()], out_specs=blk()),
        out_shape=jax.ShapeDtypeStruct((k, t_tok), jnp.int32),
        compiler_params=_cparams(("arbitrary",)),
        name="plan",
    )(pad_start, idx_t, rank_t)


def _dispatch_kernel(lo_ref, hi_ref, h2p_ref, pos_hbm, xs_hbm, pos_smem, zero_ref, sem_pos, sem_rows, *, tm):
    i = pl.program_id(0)
    pos_cp = pltpu.make_async_copy(pos_hbm.at[:, pl.ds(i * tm, tm)], pos_smem, sem_pos)
    pos_cp.start()

    @pl.when(i == 0)
    def _():
        zero_ref[...] = jnp.zeros(zero_ref.shape, jnp.uint32)
        zrows = zero_ref.shape[0]

        def zero_cp(e):
            start = pl.multiple_of(hi_ref[e] * 8 - zrows, 8)
            return pltpu.make_async_copy(zero_ref, xs_hbm.at[pl.ds(start, zrows), :], sem_rows)

        def start_one(e, c):
            @pl.when(hi_ref[e] > lo_ref[e])
            def _():
                zero_cp(e).start()
            return c

        def wait_one(e, c):
            @pl.when(hi_ref[e] > lo_ref[e])
            def _():
                zero_cp(e).wait()
            return c
        lax.fori_loop(0, N_EXPERTS, start_one, 0)
        lax.fori_loop(0, N_EXPERTS, wait_one, 0)

    pos_cp.wait()

    def per_token(r, carry):
        src = h2p_ref.at[pl.ds(pl.multiple_of(r * 8, 8), 8), :]
        for k in range(TOP_K):
            p = pos_smem[k, r]
            pltpu.make_async_copy(src, xs_hbm.at[pl.ds(pl.multiple_of(p * 8, 8), 8), :],
                                  sem_rows).start(priority=k % 2)
        return carry
    lax.fori_loop(0, tm, per_token, 0)
    for k in range(TOP_K):
        pltpu.make_async_copy(h2p_ref, xs_hbm.at[pl.ds(0, tm * 8), :], sem_rows).wait()


def _dispatch(pad_lo, pad_hi, h2p, pos_t, p_len):
    t_tok = pos_t.shape[1]
    tm = ROW_TILE
    return pl.pallas_call(
        functools.partial(_dispatch_kernel, tm=tm),
        grid_spec=pltpu.PrefetchScalarGridSpec(
            num_scalar_prefetch=2, grid=(t_tok // tm,),
            in_specs=[pl.BlockSpec((tm * 8, 128), lambda i, lo, hi: (i, 0)),
                      pl.BlockSpec(memory_space=pl.ANY)],
            out_specs=pl.BlockSpec(memory_space=pl.ANY),
            scratch_shapes=[pltpu.SMEM((TOP_K, tm), jnp.int32), pltpu.VMEM((EXPERT_TILE * 8, 128), jnp.uint32),
                            pltpu.SemaphoreType.DMA, pltpu.SemaphoreType.DMA]),
        out_shape=jax.ShapeDtypeStruct((p_len * 8, 128), jnp.uint32),
        compiler_params=_cparams(("arbitrary",)),
        name="dispatch",
    )(pad_lo, pad_hi, h2p, pos_t)


def _experts_kernel(te_ref, nu_ref, x_ref, wg_ref, wu_ref, wd_ref, y_ref):
    t = pl.program_id(0)

    @pl.when(t < nu_ref[0])
    def _():
        tm = x_ref.shape[0] // 8
        lo, hi = _load_token_tiles(x_ref, tm)
        xb = jnp.concatenate([lo, hi], axis=1).astype(BF16)
        gate = jnp.dot(xb, wg_ref[0], preferred_element_type=F32)
        up = jnp.dot(xb, wu_ref[0], preferred_element_type=F32)
        act = (gate * jax.nn.sigmoid(gate) * up).astype(BF16)
        _store_token_tiles(y_ref, jnp.dot(act, wd_ref[0], preferred_element_type=F32))


def _experts(tile_expert, n_used, xs, wg, wu, wd):
    p_len = xs.shape[0] // 8
    d, ff = wg.shape[1], wg.shape[2]
    tm = EXPERT_TILE
    n_tiles = p_len // tm
    rows = pl.BlockSpec((tm * 8, 128), lambda t, te, nu: (jnp.minimum(t, nu[0] - 1), 0))
    grid_spec = pltpu.PrefetchScalarGridSpec(
        num_scalar_prefetch=2,
        grid=(n_tiles,),
        in_specs=[rows,
                  pl.BlockSpec((1, d, ff), lambda t, te, nu: (te[t], 0, 0)),
                  pl.BlockSpec((1, d, ff), lambda t, te, nu: (te[t], 0, 0)),
                  pl.BlockSpec((1, ff, d), lambda t, te, nu: (te[t], 0, 0))],
        out_specs=rows,
    )
    return pl.pallas_call(
        _experts_kernel,
        grid_spec=grid_spec,
        out_shape=jax.ShapeDtypeStruct((p_len * 8, 128), jnp.uint32),
        compiler_params=_cparams(("arbitrary",)),
        name="experts",
    )(tile_expert, n_used, xs, wg, wu, wd)


def _final_kernel(x2_ref, w_ref, gf_ref, g_ref, pos_hbm, ys_hbm, o_ref, pos_smem, buf_ref, sem_pos, sem_rows,
                  *, tm, nsteps):
    i = pl.program_id(0)
    slot = i % 2

    def pos_copy(step, s):
        return pltpu.make_async_copy(pos_hbm.at[:, pl.ds(step * tm, tm)], pos_smem.at[s], sem_pos.at[s])

    def issue_gathers(s):
        def per_token(r, carry):
            for k in range(TOP_K):
                p = pos_smem[s, k, r]
                pltpu.make_async_copy(ys_hbm.at[pl.ds(pl.multiple_of(p * 8, 8), 8), :],
                                      buf_ref.at[s, k, pl.ds(pl.multiple_of(r * 8, 8), 8), :],
                                      sem_rows.at[s]).start(priority=k % 2)
            return carry
        lax.fori_loop(0, tm, per_token, 0)

    @pl.when(i == 0)
    def _():
        pos_copy(0, 0).start()
        pos_copy(0, 0).wait()
        issue_gathers(0)
        if nsteps > 1:
            pos_copy(1, 1).start()

    @pl.when(i + 1 < nsteps)
    def _():
        pos_copy(i + 1, 1 - slot).wait()
        issue_gathers(1 - slot)

    @pl.when(i + 2 < nsteps)
    def _():
        pos_copy(i + 2, slot).start()

    for k in range(TOP_K):
        pltpu.make_async_copy(ys_hbm.at[pl.ds(0, tm * 8), :], buf_ref.at[slot, k], sem_rows.at[slot]).wait()

    w = w_ref[...]
    d = x2_ref.shape[1]
    lo_acc = jnp.zeros((tm, d // 2), F32)
    hi_acc = jnp.zeros((tm, d // 2), F32)
    for k in range(TOP_K):
        lo, hi = _load_token_tiles(buf_ref.at[slot, k], tm)
        lo_acc = lo_acc + w[:, k:k + 1] * lo
        hi_acc = hi_acc + w[:, k:k + 1] * hi
    routed = jnp.concatenate([lo_acc, hi_acc], axis=1)
    x = x2_ref[...] + gf_ref[0] * routed
    o_ref[...] = _rms(x, g_ref[...])


def _final(x2, w128, g_f, g_final, pos_t, ys):
    b, s_len, d = x2.shape
    t_tok = b * s_len
    tm = 128
    nt = s_len // tm
    nsteps = t_tok // tm
    row = pl.BlockSpec((tm, d), lambda i: (i, 0))
    out = pl.pallas_call(
        functools.partial(_final_kernel, tm=tm, nsteps=nsteps),
        grid=(nsteps,),
        in_specs=[row, pl.BlockSpec((tm, 128), lambda i: (i, 0)),
                  pl.BlockSpec((1, 1, d), lambda i: (i // nt, 0, 0)), _const_spec((1, d)),
                  pl.BlockSpec(memory_space=pl.ANY), pl.BlockSpec(memory_space=pl.ANY)],
        out_specs=row,
        out_shape=jax.ShapeDtypeStruct((t_tok, d), F32),
        scratch_shapes=[pltpu.SMEM((2, TOP_K, tm), jnp.int32), pltpu.VMEM((2, TOP_K, tm * 8, 128), jnp.uint32),
                        pltpu.SemaphoreType.DMA((2,)), pltpu.SemaphoreType.DMA((2,))],
        compiler_params=_cparams(("arbitrary",)),
        name="final",
    )(x2.reshape(t_tok, d), w128, g_f, g_final, pos_t, ys)
    return out.reshape(b, s_len, d)


def _rope_tables(s_len, c_len):
    rows = s_len // GRID_W
    r = jnp.repeat(jnp.arange(rows, dtype=F32), GRID_W)
    cidx = jnp.tile(jnp.arange(GRID_W, dtype=F32), rows)
    n_freq = 16
    inv = ROPE_BASE ** (-jnp.arange(n_freq, dtype=F32) / n_freq)
    ang = jnp.concatenate([r[:, None] * inv, cidx[:, None] * inv], axis=-1)
    cos = jnp.concatenate([jnp.cos(ang), jnp.ones((c_len, 32), F32)], axis=0)
    sin = jnp.concatenate([jnp.sin(ang), jnp.zeros((c_len, 32), F32)], axis=0)
    rope_a = jnp.concatenate([jnp.tile(cos, (1, 4)), jnp.tile(sin, (1, 4))], axis=1)
    rope_k = jnp.concatenate([cos, cos, sin, sin], axis=1)
    rope_t = jnp.concatenate([cos.T, sin.T], axis=0)
    return rope_a, rope_k, rope_t


def _in_weight(w_in):
    qa_cols = np.empty((4, 2, 4, 32), np.int32)
    for r in range(4):
        for p in range(2):
            for j in range(4):
                qa_cols[r, p, j] = (4 * j + r) * 64 + p * 32 + np.arange(32)
    ka_cols = np.empty((2, 4, 32), np.int32)
    for p in range(2):
        for j in range(4):
            ka_cols[p, j] = 1024 + j * 64 + p * 32 + np.arange(32)
    cols = np.concatenate([qa_cols.reshape(-1), ka_cols.reshape(-1), np.arange(1280, 2368)])
    w = w_in[:, cols]
    kr = w_in[:, 2304:2368]
    kr_rot = jnp.concatenate([-kr[:, 32:], kr[:, :32]], axis=1)
    return jnp.concatenate([w, kr_rot], axis=1).astype(BF16)


def kernel(x, c, ctx, c_ctx, w_mod, b_mod, norm_attn_g, norm_ffn_g, w_in, attn_sink, q_a_norm_g, w_uq,
           kv_a_norm_g, w_ukv, w_out, w_router, router_bias, w_gate, w_up, w_down, ws_gate, ws_up, ws_down,
           norm_final_g):
    b, s_len, d = x.shape
    c_len = ctx.shape[1]
    l_total = s_len + c_len
    t_tok = b * s_len

    c_rows = jnp.zeros((8, d), F32).at[:b].set(c).at[b].set(c_ctx)
    mod = _mod(c_rows, w_mod[0], b_mod[0][None, :])
    mod6 = mod.reshape(8, 6, d)
    sh_a, sc_a, g_a, sh_f, sc_f, g_f = (mod6[:b, k] for k in range(6))
    ms_x = jnp.stack([sh_a, sc_a], axis=1)
    ms_c = jnp.broadcast_to(jnp.stack([mod6[b, 0], mod6[b, 1]], axis=0)[None], (b, 2, d))

    w_in_p = _in_weight(w_in[0])
    w_uq_h = w_uq[0].reshape(Q_LORA_RANK, B_HEADS, B_QK_DIM)
    w_uqt = jnp.pad(jnp.transpose(w_uq_h, (1, 2, 0)), ((0, 0), (0, HEAD_PAD - B_QK_DIM), (0, 0)))
    w_uqt = w_uqt.reshape(B_HEADS * HEAD_PAD, Q_LORA_RANK).astype(BF16)
    w_ukv_h = w_ukv[0].reshape(KV_LORA_RANK, B_HEADS, B_NOPE_DIM + B_V_DIM)
    w_uk = w_ukv_h[:, :, :B_NOPE_DIM].reshape(KV_LORA_RANK, B_HEADS * B_NOPE_DIM).astype(BF16)
    w_uvt = w_ukv_h[:, :, B_NOPE_DIM:].reshape(KV_LORA_RANK, B_HEADS * B_V_DIM).T.astype(BF16)
    rope_a, rope_k, rope_t = _rope_tables(s_len, c_len)

    g_attn = norm_attn_g[0][None, :]
    gq = q_a_norm_g[0][None, :]
    gkv = kv_a_norm_g[0][None, :]
    proj_w = (g_attn, w_in_p, gq, w_uqt, gkv, w_uk, w_uvt, rope_a, rope_k, rope_t)
    outs = _inproj(x, ms_x, *proj_w, l_total, 0)
    qa, ka, va, qbt, kb, vbt = _inproj(ctx, ms_c, *proj_w, l_total, s_len // ROW_TILE, prev=outs)

    stack_heads = np.array([4 * (s % 4) + s // 4 for s in range(16)], np.int32)
    sink_row = jnp.repeat(attn_sink[0][stack_heads] * LOG2E, SWA_BLOCK)[None, :].astype(F32)
    o_a = _swa(qa, ka, va, sink_row, s_len, c_len)
    o_b = _mla(qbt, kb, vbt, s_len)

    oa_rows = np.empty((4, 4, 64), np.int32)
    for r in range(4):
        for j in range(4):
            oa_rows[r, j] = (4 * j + r) * 64 + np.arange(64)
    woa = w_out[0][oa_rows.reshape(-1)].astype(BF16)
    wob = w_out[0][1024:].astype(BF16)
    wr = jnp.pad(w_router[0], ((0, 0), (0, 128 - N_EXPERTS)))
    wr_hi = wr.astype(BF16)
    wr = jnp.stack([wr_hi, (wr - wr_hi.astype(F32)).astype(BF16)], axis=0)
    mod4 = jnp.stack([g_a, sh_f, sc_f, g_f], axis=1)
    h2p, x2, idx_t, rank_t, w128, cnt = _post(
        o_a, o_b, x, mod4, norm_ffn_g[0][None, :], woa, wob, wr, router_bias[0][:, None].astype(F32),
        ws_gate[0].astype(BF16), ws_up[0].astype(BF16), ws_down[0].astype(BF16))

    tm = EXPERT_TILE
    counts = cnt[:, 0].astype(jnp.int32)
    padded = (counts + tm - 1) // tm * tm
    pad_end = jnp.cumsum(padded).astype(jnp.int32)
    pad_start = pad_end - padded
    n_tiles = -(-t_tok * TOP_K // tm) + N_EXPERTS
    tile_row0 = jnp.arange(n_tiles, dtype=jnp.int32) * tm
    tile_expert = jnp.minimum(jnp.sum((pad_end[None, :] <= tile_row0[:, None]).astype(jnp.int32), axis=1),
                              N_EXPERTS - 1)
    n_used = (pad_end[-1] // tm).astype(jnp.int32)[None]

    pos_t = _plan(pad_start, idx_t, rank_t)
    xs = _dispatch(pad_start, pad_end, h2p, pos_t, n_tiles * tm)
    ys = _experts(tile_expert, n_used, xs, w_gate[0].astype(BF16), w_up[0].astype(BF16), w_down[0].astype(BF16))
    return _final(x2, w128, g_f[:, None, :], norm_final_g[None, :], pos_t, ys)
```

```python
import functools
import math

import numpy as np
import jax
import jax.numpy as jnp
from jax import lax
from jax.experimental import pallas as pl
from jax.experimental.pallas import tpu as pltpu

F32 = jnp.float32
BF16 = jnp.bfloat16

GRID_W = 64
ROPE_BASE = 10000.0
EPS = 1e-6
A_HEADS = 16
A_KV_HEADS = 4
A_HEAD_DIM = 64
WINDOW = 128
B_HEADS = 8
B_NOPE_DIM = 128
B_ROPE_DIM = 64
B_QK_DIM = B_NOPE_DIM + B_ROPE_DIM
B_V_DIM = 128
Q_LORA_RANK = 512
KV_LORA_RANK = 256
N_EXPERTS = 64
TOP_K = 8
N_GROUPS = 8
TOPK_GROUPS = 4
ROUTED_SCALE = 2.5

LOG2E = math.log2(math.e)
NEG_BIG = -1e30

VMEM_LIMIT_BYTES = 56 * 1024 * 1024

ROW_TILE = 256
SWA_BLOCK = 128
MLA_TQ = 1024
MLA_TK = 768
MLA_KC = 128
EXPERT_TILE = 256
FINAL_TILE = 128
FINAL_CHUNK = 16
HEAD_PAD = 256
V_ROWS = 144

IN_COLS_PADDED = 2432


def _cparams(sem):
    return pltpu.CompilerParams(dimension_semantics=sem, vmem_limit_bytes=VMEM_LIMIT_BYTES)


def _const_spec(shape):
    n = len(shape)
    return pl.BlockSpec(shape, lambda *_: (0,) * n, pipeline_mode=pl.Buffered(1))


def _mod_kernel(c_ref, w_ref, b_ref, o_ref):
    c = c_ref[...]
    a = c * jax.nn.sigmoid(c)
    o_ref[...] = jnp.dot(a, w_ref[...], preferred_element_type=F32,
                         precision=lax.Precision.HIGHEST) + b_ref[...]


def _mod(c_rows, w_mod, b_mod):
    d, n = w_mod.shape
    tn = 1024
    return pl.pallas_call(
        _mod_kernel,
        grid=(n // tn,),
        in_specs=[pl.BlockSpec((8, d), lambda j: (0, 0)),
                  pl.BlockSpec((d, tn), lambda j: (0, j)),
                  pl.BlockSpec((1, tn), lambda j: (0, j))],
        out_specs=pl.BlockSpec((8, tn), lambda j: (0, j)),
        out_shape=jax.ShapeDtypeStruct((8, n), F32),
        compiler_params=_cparams(("arbitrary",)),
        name="mod",
    )(c_rows, w_mod, b_mod)


def _rms(x, g):
    var = jnp.mean(x * x, axis=-1, keepdims=True)
    return x * lax.rsqrt(var + EPS) * g


def _inproj_kernel(x_ref, ms_ref, g_ref, w_in_ref, gq_ref, w_uqt_ref, gkv_ref, w_uk_ref, w_uvt_ref,
                   rope_a_ref, rope_k_ref, rope_t_ref, *rest, sa, sb):
    qa_ref, ka_ref, va_ref, qbt_ref, kb_ref, vbt_ref = rest[-6:]
    x = x_ref[0]
    shift = ms_ref[0, 0:1, :]
    scale = ms_ref[0, 1:2, :]
    h = _rms(x, g_ref[...]) * (1.0 + scale) + shift
    z = jnp.dot(h.astype(BF16), w_in_ref[...], preferred_element_type=F32)

    cos_a = rope_a_ref[:, 0:128]
    sin_a = rope_a_ref[:, 128:256]
    for r in range(4):
        q1 = z[:, 256 * r:256 * r + 128]
        q2 = z[:, 256 * r + 128:256 * r + 256]
        qa_ref[0, 256 * r:256 * r + 128, :] = ((q1 * cos_a - q2 * sin_a) * sa).T.astype(BF16)
        qa_ref[0, 256 * r + 128:256 * r + 256, :] = ((q2 * cos_a + q1 * sin_a) * sa).T.astype(BF16)
    k1 = z[:, 1024:1152]
    k2 = z[:, 1152:1280]
    ka_ref[0, :, 0:128] = (k1 * cos_a - k2 * sin_a).astype(BF16)
    ka_ref[0, :, 128:256] = (k2 * cos_a + k1 * sin_a).astype(BF16)
    va_ref[0] = z[:, 1280:1536].T.astype(BF16)

    cqn = _rms(z[:, 1536:2048], gq_ref[...])
    qt = jnp.dot(w_uqt_ref[...], cqn.T.astype(BF16), preferred_element_type=F32)
    cos_t = rope_t_ref[0:32, :]
    sin_t = rope_t_ref[32:64, :]
    for hh in range(B_HEADS):
        base = HEAD_PAD * hh
        x1 = qt[base + 128:base + 160]
        x2 = qt[base + 160:base + 192]
        qbt_ref[0, hh, 0, 0:128, :] = (qt[base:base + 128] * sb).astype(BF16)
        qbt_ref[0, hh, 0, 128:160, :] = ((x1 * cos_t - x2 * sin_t) * sb).astype(BF16)
        qbt_ref[0, hh, 0, 160:192, :] = ((x2 * cos_t + x1 * sin_t) * sb).astype(BF16)
        qbt_ref[0, hh, 0, 192:256, :] = jnp.zeros((64, x.shape[0]), BF16)

    ckvn = _rms(z[:, 2048:2304], gkv_ref[...])
    knope = jnp.dot(ckvn.astype(BF16), w_uk_ref[...], preferred_element_type=F32)
    vt = jnp.dot(w_uvt_ref[...], ckvn.T.astype(BF16), preferred_element_type=F32)
    t = z[:, 2304:2432] * rope_k_ref[...]
    lane = lax.broadcasted_iota(jnp.int32, t.shape, 1)
    krp = jnp.where(lane < 64, t + pltpu.roll(t, 64, 1), 0.0).astype(BF16)
    for hh in range(B_HEADS):
        vbt_ref[0, hh, 0:B_V_DIM, :] = vt[B_V_DIM * hh:B_V_DIM * (hh + 1)].astype(BF16)
        vbt_ref[0, hh, B_V_DIM:V_ROWS, :] = jnp.ones((V_ROWS - B_V_DIM, x.shape[0]), BF16)
        kb_ref[0, hh, :, 0:128] = knope[:, 128 * hh:128 * hh + 128].astype(BF16)
        kb_ref[0, hh, :, 128:256] = krp


def _inproj(x, ms, g_attn, w_in_p, gq, w_uqt, gkv, w_uk, w_uvt, rope_a, rope_k, rope_t,
            l_total, row_block0, prev=None):
    b, n, d = x.shape
    tm = ROW_TILE
    nt = n // tm
    sa = (A_HEAD_DIM ** -0.5) * LOG2E
    sb = (B_QK_DIM ** -0.5) * LOG2E
    out_shape = [
        jax.ShapeDtypeStruct((b, A_HEADS * A_HEAD_DIM, l_total), BF16),
        jax.ShapeDtypeStruct((b, l_total, A_KV_HEADS * A_HEAD_DIM), BF16),
        jax.ShapeDtypeStruct((b, A_KV_HEADS * A_HEAD_DIM, l_total), BF16),
        jax.ShapeDtypeStruct((b, B_HEADS, l_total // tm, HEAD_PAD, tm), BF16),
        jax.ShapeDtypeStruct((b, B_HEADS, l_total, HEAD_PAD), BF16),
        jax.ShapeDtypeStruct((b, B_HEADS, V_ROWS, l_total), BF16),
    ]
    r0 = row_block0
    out_specs = [
        pl.BlockSpec((1, 1024, tm), lambda bi, i: (bi, 0, r0 + i)),
        pl.BlockSpec((1, tm, 256), lambda bi, i: (bi, r0 + i, 0)),
        pl.BlockSpec((1, 256, tm), lambda bi, i: (bi, 0, r0 + i)),
        pl.BlockSpec((1, B_HEADS, 1, HEAD_PAD, tm), lambda bi, i: (bi, 0, r0 + i, 0, 0)),
        pl.BlockSpec((1, B_HEADS, tm, HEAD_PAD), lambda bi, i: (bi, 0, r0 + i, 0)),
        pl.BlockSpec((1, B_HEADS, V_ROWS, tm), lambda bi, i: (bi, 0, 0, r0 + i)),
    ]
    in_specs = [
        pl.BlockSpec((1, tm, d), lambda bi, i: (bi, i, 0)),
        pl.BlockSpec((1, 2, d), lambda bi, i: (bi, 0, 0)),
        _const_spec((1, d)),
        _const_spec(w_in_p.shape),
        _const_spec((1, Q_LORA_RANK)),
        _const_spec(w_uqt.shape),
        _const_spec((1, KV_LORA_RANK)),
        _const_spec(w_uk.shape),
        _const_spec(w_uvt.shape),
        pl.BlockSpec((tm, 256), lambda bi, i: (r0 + i, 0)),
        pl.BlockSpec((tm, 128), lambda bi, i: (r0 + i, 0)),
        pl.BlockSpec((64, tm), lambda bi, i: (0, r0 + i)),
    ]
    args = [x, ms, g_attn, w_in_p, gq, w_uqt, gkv, w_uk, w_uvt, rope_a, rope_k, rope_t]
    aliases = {}
    if prev is not None:
        for k, p in enumerate(prev):
            in_specs.append(pl.BlockSpec(memory_space=pl.ANY))
            aliases[len(args)] = k
            args.append(p)
    return pl.pallas_call(
        functools.partial(_inproj_kernel, sa=sa, sb=sb),
        grid=(b, nt),
        in_specs=in_specs,
        out_specs=out_specs,
        out_shape=out_shape,
        input_output_aliases=aliases,
        compiler_params=_cparams(("arbitrary", "arbitrary")),
        name="inproj_ctx" if prev is not None else "inproj",
    )(*args)


def _swa_kernel(qt_ref, kp_ref, kc_ref, kn_ref, vp_ref, vc_ref, vn_ref, kx_ref, vx_ref, sink_ref, o_ref,
                s0, s1, bm0, bm1, p0, p1, d0, d1, *, nblk):
    i = pl.program_id(1)
    s_refs, bm_refs, p_refs, d_refs = (s0, s1), (bm0, bm1), (p0, p1), (d0, d1)
    nwin = 3 * SWA_BLOCK
    kall = jnp.concatenate([kp_ref[0], kc_ref[0], kn_ref[0], kx_ref[0]], axis=0)
    vall = jnp.concatenate([vp_ref[0], vc_ref[0], vn_ref[0], vx_ref[0]], axis=1)
    ck = lax.broadcasted_iota(jnp.int32, (nwin, SWA_BLOCK), 0)
    rq = lax.broadcasted_iota(jnp.int32, (nwin, SWA_BLOCK), 1)
    band = jnp.abs(ck - WINDOW - rq) <= WINDOW
    band = band & ((ck >= SWA_BLOCK) | (i > 0)) & ((ck < 2 * SWA_BLOCK) | (i < nblk - 1))
    band4 = jnp.concatenate([band] * A_KV_HEADS, axis=1)

    def scores(c):
        q1 = qt_ref[0, 256 * c:256 * c + 128, :]
        q2 = qt_ref[0, 256 * c + 128:256 * c + 256, :]
        cols = []
        for j in range(A_KV_HEADS):
            rows = slice(32 * j, 32 * j + 32)
            pieces = []
            for half in (q1, q2):
                if j > 0:
                    pieces.append(jnp.zeros((32 * j, SWA_BLOCK), BF16))
                pieces.append(half[rows])
                if j < A_KV_HEADS - 1:
                    pieces.append(jnp.zeros((96 - 32 * j, SWA_BLOCK), BF16))
            cols.append(jnp.concatenate(pieces, axis=0))
        st = jnp.dot(kall, jnp.concatenate(cols, axis=1), preferred_element_type=F32)
        st = jnp.concatenate([jnp.where(band4, st[:nwin], NEG_BIG), st[nwin:]], axis=0)
        s_refs[c % 2][...] = st
        bm_refs[c % 2][...] = jnp.max(st, axis=0, keepdims=True)

    def softmax(c):
        sink = sink_ref[:, 512 * c:512 * (c + 1)]
        m = jnp.maximum(bm_refs[c % 2][...], sink)
        p = jnp.exp2(s_refs[c % 2][...] - m)
        d_refs[c % 2][...] = jnp.sum(p, axis=0, keepdims=True) + jnp.exp2(sink - m)
        p_refs[c % 2][...] = p.astype(BF16)

    def values(c):
        ot = jnp.dot(vall, p_refs[c % 2][...], preferred_element_type=F32)
        ot = ot * (1.0 / d_refs[c % 2][...])
        oc = jnp.concatenate([ot[64 * j:64 * j + 64, SWA_BLOCK * j:SWA_BLOCK * (j + 1)]
                              for j in range(A_KV_HEADS)], axis=0)
        o_ref[0, :, 256 * c:256 * c + 256] = oc.T.astype(BF16)

    n_it = A_HEADS // A_KV_HEADS
    scores(0)
    for c in range(n_it):
        if c + 1 < n_it:
            scores(c + 1)
        if c >= 1:
            values(c - 1)
        softmax(c)
    values(n_it - 1)


def _swa(qat, ka, vat, sink_row, s_len, c_len):
    b = qat.shape[0]
    nblk = s_len // SWA_BLOCK
    nkeys = 3 * SWA_BLOCK + c_len
    krow = lambda f: pl.BlockSpec((1, SWA_BLOCK, 256), f)
    vcol = lambda f: pl.BlockSpec((1, 256, SWA_BLOCK), f)
    prev_r = lambda bi, i: (bi, jnp.maximum(i - 1, 0), 0)
    cur_r = lambda bi, i: (bi, i, 0)
    next_r = lambda bi, i: (bi, jnp.minimum(i + 1, nblk - 1), 0)
    prev_c = lambda bi, i: (bi, 0, jnp.maximum(i - 1, 0))
    cur_c = lambda bi, i: (bi, 0, i)
    next_c = lambda bi, i: (bi, 0, jnp.minimum(i + 1, nblk - 1))
    two = lambda shape, dt: [pltpu.VMEM(shape, dt), pltpu.VMEM(shape, dt)]
    return pl.pallas_call(
        functools.partial(_swa_kernel, nblk=nblk),
        grid=(b, nblk),
        in_specs=[pl.BlockSpec((1, 1024, SWA_BLOCK), cur_c),
                  krow(prev_r), krow(cur_r), krow(next_r), vcol(prev_c), vcol(cur_c), vcol(next_c),
                  pl.BlockSpec((1, c_len, 256), lambda bi, i: (bi, s_len // c_len, 0)),
                  pl.BlockSpec((1, 256, c_len), lambda bi, i: (bi, 0, s_len // c_len)),
                  _const_spec(sink_row.shape)],
        out_specs=pl.BlockSpec((1, SWA_BLOCK, 1024), cur_r),
        out_shape=jax.ShapeDtypeStruct((b, s_len, 1024), BF16),
        scratch_shapes=two((nkeys, 512), F32) + two((1, 512), F32) + two((nkeys, 512), BF16) + two((1, 512), F32),
        compiler_params=_cparams(("arbitrary", "arbitrary")),
        name="swa",
    )(qat, ka, ka, ka, vat, vat, vat, ka, vat, sink_row)


def _mla_kernel(qt_ref, k_ref, vt_ref, o_ref, m_ref, acc_ref, s0, s1, bm0, bm1, p0, p1, a0, a1, *, nk, nc):
    kj = pl.program_id(2)
    tk = k_ref.shape[2]
    qc = qt_ref.shape[4]
    n_it = B_HEADS * nc
    s_refs, bm_refs, p_refs, a_refs = (s0, s1), (bm0, bm1), (p0, p1), (a0, a1)

    @pl.when(kj == 0)
    def _():
        m_ref[...] = jnp.full(m_ref.shape, NEG_BIG, F32)
        acc_ref[...] = jnp.zeros(acc_ref.shape, F32)

    def scores(i):
        h, c = divmod(i, nc)
        st = jnp.dot(k_ref[0, h], qt_ref[0, h, c], preferred_element_type=F32)
        s_refs[i % 2][...] = st
        bm_refs[i % 2][...] = jnp.max(st, axis=0, keepdims=True)

    def softmax(i):
        h, c = divmod(i, nc)
        m_old = m_ref[h, c]
        m_new = jnp.maximum(m_old, bm_refs[i % 2][...])
        m_ref[h, c] = m_new
        a_refs[i % 2][...] = jnp.exp2(m_old - m_new)
        for r in range(tk // MLA_KC):
            rows = slice(MLA_KC * r, MLA_KC * (r + 1))
            p_refs[i % 2][rows, :] = jnp.exp2(s_refs[i % 2][rows, :] - m_new).astype(BF16)

    def values(i):
        h, c = divmod(i, nc)
        pv = jnp.dot(vt_ref[0, h], p_refs[i % 2][...], preferred_element_type=F32)
        acc_ref[h, c] = a_refs[i % 2][...] * acc_ref[h, c] + pv

    scores(0)
    for i in range(n_it):
        if i + 1 < n_it:
            scores(i + 1)
        if i >= 1:
            values(i - 1)
        softmax(i)
    values(n_it - 1)

    @pl.when(kj == nk - 1)
    def _():
        for hh in range(B_HEADS):
            for c in range(nc):
                o = acc_ref[hh, c, 0:B_V_DIM, :] * (1.0 / acc_ref[hh, c, B_V_DIM:B_V_DIM + 1, :])
                o_ref[0, qc * c:qc * (c + 1), B_V_DIM * hh:B_V_DIM * (hh + 1)] = o.T.astype(BF16)


def _mla(qbt, kb, vbt, s_len):
    b, _, _, _, qc = qbt.shape
    l_total = kb.shape[2]
    tq = min(MLA_TQ, s_len)
    tk = MLA_TK if l_total % MLA_TK == 0 else 256
    nc = tq // qc
    nq = s_len // tq
    nk = l_total // tk
    two = lambda shape, dt: [pltpu.VMEM(shape, dt), pltpu.VMEM(shape, dt)]
    return pl.pallas_call(
        functools.partial(_mla_kernel, nk=nk, nc=nc),
        grid=(b, nq, nk),
        in_specs=[pl.BlockSpec((1, B_HEADS, nc, HEAD_PAD, qc), lambda bi, qi, kj: (bi, 0, qi, 0, 0)),
                  pl.BlockSpec((1, B_HEADS, tk, HEAD_PAD), lambda bi, qi, kj: (bi, 0, kj, 0)),
                  pl.BlockSpec((1, B_HEADS, V_ROWS, tk), lambda bi, qi, kj: (bi, 0, 0, kj))],
        out_specs=pl.BlockSpec((1, tq, B_HEADS * B_V_DIM), lambda bi, qi, kj: (bi, qi, 0)),
        out_shape=jax.ShapeDtypeStruct((b, s_len, B_HEADS * B_V_DIM), BF16),
        scratch_shapes=[pltpu.VMEM((B_HEADS, nc, 1, qc), F32), pltpu.VMEM((B_HEADS, nc, V_ROWS, qc), F32)]
                       + two((tk, qc), F32) + two((1, qc), F32) + two((tk, qc), BF16) + two((1, qc), F32),
        compiler_params=_cparams(("arbitrary", "arbitrary", "arbitrary")),
        name="mla",
    )(qbt, kb, vbt)


def _pack_pairs(lo, hi):
    lo_w = lax.bitcast_convert_type(lo.astype(BF16).astype(F32), jnp.uint32) >> 16
    hi_w = lax.bitcast_convert_type(hi.astype(BF16).astype(F32), jnp.uint32) & jnp.uint32(0xFFFF0000)
    return lo_w | hi_w


def _store_token_tiles(ref, v):
    n, d = v.shape
    words = _pack_pairs(v[:, :d // 2], v[:, d // 2:])
    for s in range(8):
        ref[pl.ds(s, n, stride=8), :] = words[:, 128 * s:128 * (s + 1)]


def _load_token_tiles(ref, n):
    lo, hi = [], []
    for s in range(8):
        w = ref[pl.ds(s, n, stride=8), :]
        lo.append(lax.bitcast_convert_type(w << 16, F32))
        hi.append(lax.bitcast_convert_type(w & jnp.uint32(0xFFFF0000), F32))
    return jnp.concatenate(lo, axis=1), jnp.concatenate(hi, axis=1)


def _argmax_rows(v, iota, n):
    m = jnp.max(v, axis=0, keepdims=True)
    i = jnp.min(jnp.where(v == m, iota, n), axis=0, keepdims=True)
    return m, i


def _route_tile(logits, bias_col):
    tm = logits.shape[0]
    gsz = N_EXPERTS // N_GROUPS
    s = jax.nn.sigmoid(logits.T[0:N_EXPERTS])
    sel = s + bias_col
    ninf = -jnp.inf
    g3 = sel.reshape(N_GROUPS, gsz, tm)
    w_iota = lax.broadcasted_iota(jnp.int32, g3.shape, 1).astype(F32)
    m1 = jnp.max(g3, axis=1, keepdims=True)
    i1 = jnp.min(jnp.where(g3 == m1, w_iota, float(gsz)), axis=1, keepdims=True)
    m2 = jnp.max(jnp.where(w_iota == i1, ninf, g3), axis=1, keepdims=True)
    gs = (m1 + m2).reshape(N_GROUPS, tm)
    g_iota = lax.broadcasted_iota(jnp.int32, gs.shape, 0).astype(F32)
    keep = jnp.zeros(gs.shape, F32)
    cur = gs
    for _ in range(TOPK_GROUPS):
        _, gi = _argmax_rows(cur, g_iota, float(N_GROUPS))
        hit = g_iota == gi
        keep = jnp.where(hit, 1.0, keep)
        cur = jnp.where(hit, ninf, cur)
    cur = jnp.where(keep.reshape(N_GROUPS, 1, tm) > 0.5, g3, ninf).reshape(N_EXPERTS, tm)
    e_iota = lax.broadcasted_iota(jnp.int32, cur.shape, 0).astype(F32)
    idx, wts = [], []
    hits = jnp.zeros(cur.shape, F32)
    for _ in range(TOP_K):
        _, ei = _argmax_rows(cur, e_iota, float(N_EXPERTS))
        hit = e_iota == ei
        idx.append(ei)
        wts.append(jnp.sum(jnp.where(hit, s, 0.0), axis=0, keepdims=True))
        hits = jnp.where(hit, 1.0, hits)
        cur = jnp.where(hit, ninf, cur)
    idx = jnp.concatenate(idx, axis=0)
    w = jnp.concatenate(wts, axis=0)
    w = w / jnp.sum(w, axis=0, keepdims=True) * ROUTED_SCALE
    return idx, w, hits


def _post_kernel(oa_ref, ob_ref, x_ref, mod_ref, g_ref, woa_ref, wob_ref, wr_ref, bias_ref, wsg_ref, wsu_ref,
                 wsd_ref, h2p_ref, x2_ref, idx_ref, rank_ref, w_ref, cnt_ref, run_ref):
    first = (pl.program_id(0) == 0) & (pl.program_id(1) == 0)

    @pl.when(first)
    def _():
        run_ref[...] = jnp.zeros(run_ref.shape, F32)

    att = (jnp.dot(oa_ref[0], woa_ref[...], preferred_element_type=F32)
           + jnp.dot(ob_ref[0], wob_ref[...], preferred_element_type=F32))
    g_a = mod_ref[0, 0:1, :]
    shift = mod_ref[0, 1:2, :]
    scale = mod_ref[0, 2:3, :]
    g_f = mod_ref[0, 3:4, :]
    xn = x_ref[0] + g_a * att
    h2 = _rms(xn, g_ref[...]) * (1.0 + scale) + shift
    h2b = h2.astype(BF16)
    _store_token_tiles(h2p_ref, h2)
    tm = h2.shape[0]

    h2_lo = (h2 - h2b.astype(F32)).astype(BF16)
    logits = (jnp.dot(h2b, wr_ref[0], preferred_element_type=F32)
              + jnp.dot(h2_lo, wr_ref[0], preferred_element_type=F32)
              + jnp.dot(h2b, wr_ref[1], preferred_element_type=F32))
    idx, w, hits = _route_tile(logits, bias_ref[...])
    idx_ref[...] = idx.astype(jnp.int32)
    w_ref[...] = jnp.concatenate([w, jnp.zeros((128 - TOP_K, tm), F32)], axis=0).T
    r_i = lax.broadcasted_iota(jnp.int32, (tm, tm), 0)
    c_i = lax.broadcasted_iota(jnp.int32, (tm, tm), 1)
    before = jnp.where(r_i < c_i, 1.0, 0.0).astype(BF16)
    hb = hits.astype(BF16)
    rank = jnp.dot(hb, before, preferred_element_type=F32) + run_ref[:, 0:1]
    e_iota = lax.broadcasted_iota(jnp.int32, rank.shape, 0).astype(F32)
    rows = [jnp.sum(jnp.where(e_iota == idx[k:k + 1], rank, 0.0), axis=0, keepdims=True) for k in range(TOP_K)]
    rank_ref[...] = jnp.concatenate(rows, axis=0).astype(jnp.int32)
    run_ref[...] += jnp.dot(hb, jnp.ones((tm, 128), BF16), preferred_element_type=F32)
    cnt_ref[...] = run_ref[...]

    gate = jnp.dot(h2b, wsg_ref[...], preferred_element_type=F32)
    up = jnp.dot(h2b, wsu_ref[...], preferred_element_type=F32)
    act = (gate * jax.nn.sigmoid(gate) * up).astype(BF16)
    shared = jnp.dot(act, wsd_ref[...], preferred_element_type=F32)
    x2_ref[0] = xn + g_f * shared


def _post(oa, ob, x, mod4, g_ffn, woa, wob, wr, bias_col, wsg, wsu, wsd):
    b, s_len, d = x.shape
    tm = ROW_TILE
    nt = s_len // tm
    t_tok = b * s_len
    row = lambda w: pl.BlockSpec((1, tm, w), lambda bi, i: (bi, i, 0))
    tok_t = pl.BlockSpec((TOP_K, tm), lambda bi, i: (0, bi * nt + i))
    return pl.pallas_call(
        _post_kernel,
        grid=(b, nt),
        in_specs=[row(1024), row(1024), row(d),
                  pl.BlockSpec((1, 4, d), lambda bi, i: (bi, 0, 0)),
                  _const_spec((1, d)), _const_spec(woa.shape), _const_spec(wob.shape), _const_spec(wr.shape),
                  _const_spec(bias_col.shape),
                  _const_spec(wsg.shape), _const_spec(wsu.shape), _const_spec(wsd.shape)],
        out_specs=[pl.BlockSpec((tm * 8, 128), lambda bi, i: (bi * nt + i, 0)),
                   row(d), tok_t, tok_t,
                   pl.BlockSpec((tm, 128), lambda bi, i: (bi * nt + i, 0)),
                   pl.BlockSpec((N_EXPERTS, 128), lambda bi, i: (0, 0))],
        out_shape=[jax.ShapeDtypeStruct((t_tok * 8, 128), jnp.uint32),
                   jax.ShapeDtypeStruct((b, s_len, d), F32),
                   jax.ShapeDtypeStruct((TOP_K, t_tok), jnp.int32),
                   jax.ShapeDtypeStruct((TOP_K, t_tok), jnp.int32),
                   jax.ShapeDtypeStruct((t_tok, 128), F32),
                   jax.ShapeDtypeStruct((N_EXPERTS, 128), F32)],
        scratch_shapes=[pltpu.VMEM((N_EXPERTS, 128), F32)],
        compiler_params=_cparams(("arbitrary", "arbitrary")),
        name="post",
    )(oa, ob, x, mod4, g_ffn, woa, wob, wr, bias_col, wsg, wsu, wsd)


def _plan_kernel(ps_ref, idx_ref, rank_ref, pos_ref):
    idx = idx_ref[...]
    base = jnp.zeros(idx.shape, jnp.int32)
    for e in range(N_EXPERTS):
        base = jnp.where(idx == e, ps_ref[e], base)
    pos_ref[...] = base + rank_ref[...]


def _plan(pad_start, idx_t, rank_t):
    k, t_tok = idx_t.shape
    tn = min(2048, t_tok)
    blk = lambda: pl.BlockSpec((k, tn), lambda j, ps: (0, j))
    return pl.pallas_call(
        _plan_kernel,
        grid_spec=pltpu.PrefetchScalarGridSpec(num_scalar_prefetch=1, grid=(t_tok // tn,),
                                               in_specs=[blk(), blk()], out_specs=blk()),
        out_shape=jax.ShapeDtypeStruct((k, t_tok), jnp.int32),
        compiler_params=_cparams(("arbitrary",)),
        name="plan",
    )(pad_start, idx_t, rank_t)


def _dispatch_kernel(lo_ref, hi_ref, h2p_ref, pos_hbm, xs_hbm, pos_smem, zero_ref, sem_pos, sem_rows, *, tm):
    i = pl.program_id(0)
    pos_cp = pltpu.make_async_copy(pos_hbm.at[:, pl.ds(i * tm, tm)], pos_smem, sem_pos)
    pos_cp.start()

    @pl.when(i == 0)
    def _():
        zero_ref[...] = jnp.zeros(zero_ref.shape, jnp.uint32)
        zrows = zero_ref.shape[0]

        def zero_cp(e):
            start = pl.multiple_of(hi_ref[e] * 8 - zrows, 8)
            return pltpu.make_async_copy(zero_ref, xs_hbm.at[pl.ds(start, zrows), :], sem_rows)

        def start_one(e, c):
            @pl.when(hi_ref[e] > lo_ref[e])
            def _():
                zero_cp(e).start()
            return c

        def wait_one(e, c):
            @pl.when(hi_ref[e] > lo_ref[e])
            def _():
                zero_cp(e).wait()
            return c
        lax.fori_loop(0, N_EXPERTS, start_one, 0)
        lax.fori_loop(0, N_EXPERTS, wait_one, 0)

    pos_cp.wait()

    def per_token(r, carry):
        src = h2p_ref.at[pl.ds(pl.multiple_of(r * 8, 8), 8), :]
        for k in range(TOP_K):
            p = pos_smem[k, r]
            pltpu.make_async_copy(src, xs_hbm.at[pl.ds(pl.multiple_of(p * 8, 8), 8), :],
                                  sem_rows).start(priority=k % 2)
        return carry
    lax.fori_loop(0, tm, per_token, 0)
    for k in range(TOP_K):
        pltpu.make_async_copy(h2p_ref, xs_hbm.at[pl.ds(0, tm * 8), :], sem_rows).wait()


def _dispatch(pad_lo, pad_hi, h2p, pos_t, p_len):
    t_tok = pos_t.shape[1]
    tm = ROW_TILE
    return pl.pallas_call(
        functools.partial(_dispatch_kernel, tm=tm),
        grid_spec=pltpu.PrefetchScalarGridSpec(
            num_scalar_prefetch=2, grid=(t_tok // tm,),
            in_specs=[pl.BlockSpec((tm * 8, 128), lambda i, lo, hi: (i, 0)),
                      pl.BlockSpec(memory_space=pl.ANY)],
            out_specs=pl.BlockSpec(memory_space=pl.ANY),
            scratch_shapes=[pltpu.SMEM((TOP_K, tm), jnp.int32), pltpu.VMEM((EXPERT_TILE * 8, 128), jnp.uint32),
                            pltpu.SemaphoreType.DMA, pltpu.SemaphoreType.DMA]),
        out_shape=jax.ShapeDtypeStruct((p_len * 8, 128), jnp.uint32),
        compiler_params=_cparams(("arbitrary",)),
        name="dispatch",
    )(pad_lo, pad_hi, h2p, pos_t)


def _experts_kernel(te_ref, nu_ref, x_ref, wg_ref, wu_ref, wd_ref, y_ref, wg_s, wu_s, wd_s):
    t = pl.program_id(0)

    @pl.when(t < nu_ref[0])
    def _():
        @pl.when((t == 0) | (te_ref[t] != te_ref[jnp.maximum(t - 1, 0)]))
        def _():
            wg_s[...] = wg_ref[0].astype(BF16)
            wu_s[...] = wu_ref[0].astype(BF16)
            wd_s[...] = wd_ref[0].astype(BF16)

        tm = x_ref.shape[0] // 8
        lo, hi = _load_token_tiles(x_ref, tm)
        xb = jnp.concatenate([lo, hi], axis=1).astype(BF16)
        gate = jnp.dot(xb, wg_s[...], preferred_element_type=F32)
        up = jnp.dot(xb, wu_s[...], preferred_element_type=F32)
        act = (gate * jax.nn.sigmoid(gate) * up).astype(BF16)
        _store_token_tiles(y_ref, jnp.dot(act, wd_s[...], preferred_element_type=F32))


def _experts(tile_expert, n_used, xs, wg, wu, wd):
    p_len = xs.shape[0] // 8
    d, ff = wg.shape[1], wg.shape[2]
    tm = EXPERT_TILE
    n_tiles = p_len // tm
    rows = pl.BlockSpec((tm * 8, 128), lambda t, te, nu: (jnp.minimum(t, nu[0] - 1), 0))
    grid_spec = pltpu.PrefetchScalarGridSpec(
        num_scalar_prefetch=2,
        grid=(n_tiles,),
        in_specs=[rows,
                  pl.BlockSpec((1, d, ff), lambda t, te, nu: (te[t], 0, 0)),
                  pl.BlockSpec((1, d, ff), lambda t, te, nu: (te[t], 0, 0)),
                  pl.BlockSpec((1, ff, d), lambda t, te, nu: (te[t], 0, 0))],
        out_specs=rows,
        scratch_shapes=[pltpu.VMEM((d, ff), BF16), pltpu.VMEM((d, ff), BF16), pltpu.VMEM((ff, d), BF16)],
    )
    return pl.pallas_call(
        _experts_kernel,
        grid_spec=grid_spec,
        out_shape=jax.ShapeDtypeStruct((p_len * 8, 128), jnp.uint32),
        compiler_params=_cparams(("arbitrary",)),
        name="experts",
    )(tile_expert, n_used, xs, wg, wu, wd)


def _final_kernel(x2_ref, w_ref, gf_ref, g_ref, pos_hbm, ys_hbm, o_ref, pos_smem, buf0, buf1, buf2, sem_pos, sem_rows,
                  *, tm, nsteps):
    i = pl.program_id(0)
    bufs = (buf0, buf1, buf2)
    ch = FINAL_CHUNK
    d = x2_ref.shape[1]

    def pos_copy(step, s):
        return pltpu.make_async_copy(pos_hbm.at[:, pl.ds(step * tm, tm)], pos_smem.at[s], sem_pos.at[s])

    def start_gathers(s, r):
        for k in range(TOP_K):
            p = pos_smem[s, k, r]
            pltpu.make_async_copy(ys_hbm.at[pl.ds(pl.multiple_of(p * 8, 8), 8), :],
                                  bufs[s].at[k, pl.ds(pl.multiple_of(r * 8, 8), 8), :],
                                  sem_rows.at[s]).start(priority=k % 2)

    def start_tile(s):
        def per_token(r, carry):
            start_gathers(s, r)
            return carry
        lax.fori_loop(0, tm, per_token, 0)

    @pl.when(i == 0)
    def _():
        for t in range(min(2, nsteps)):
            pos_copy(t, t).start()
        for t in range(min(2, nsteps)):
            pos_copy(t, t).wait()
            start_tile(t)
        if nsteps > 2:
            pos_copy(2, 2).start()

    def combine(s, j):
        r0 = pl.multiple_of(j * ch, ch)
        w = w_ref[pl.ds(r0, ch), :]
        lo_acc = jnp.zeros((ch, d // 2), F32)
        hi_acc = jnp.zeros((ch, d // 2), F32)
        for k in range(TOP_K):
            rows = bufs[s].at[k, pl.ds(pl.multiple_of(r0 * 8, 8 * ch), ch * 8), :]
            lo, hi = _load_token_tiles(rows, ch)
            lo_acc = lo_acc + w[:, k:k + 1] * lo
            hi_acc = hi_acc + w[:, k:k + 1] * hi
        x = x2_ref[pl.ds(r0, ch), :] + gf_ref[0] * jnp.concatenate([lo_acc, hi_acc], axis=1)
        o_ref[pl.ds(r0, ch), :] = _rms(x, g_ref[...])

    def step(s):
        a = (s + 2) % 3

        @pl.when(i + 2 < nsteps)
        def _():
            pos_copy(i + 2, a).wait()

        @pl.when(i + 3 < nsteps)
        def _():
            pos_copy(i + 3, s).start()

        for k in range(TOP_K):
            pltpu.make_async_copy(ys_hbm.at[pl.ds(0, tm * 8), :], bufs[s].at[k], sem_rows.at[s]).wait()

        @pl.when(i + 2 < nsteps)
        def _():
            def body(j, carry):
                for r in range(ch):
                    start_gathers(a, j * ch + r)
                combine(s, j)
                return carry
            lax.fori_loop(0, tm // ch, body, 0)

        @pl.when(i + 2 >= nsteps)
        def _():
            def body(j, carry):
                combine(s, j)
                return carry
            lax.fori_loop(0, tm // ch, body, 0)

    for s in range(3):
        @pl.when(lax.rem(i, 3) == s)
        def _(s=s):
            step(s)


def _final(x2, w128, g_f, g_final, pos_t, ys):
    b, s_len, d = x2.shape
    t_tok = b * s_len
    tm = FINAL_TILE
    nt = s_len // tm
    nsteps = t_tok // tm
    row = pl.BlockSpec((tm, d), lambda i: (i, 0))
    out = pl.pallas_call(
        functools.partial(_final_kernel, tm=tm, nsteps=nsteps),
        grid=(nsteps,),
        in_specs=[row, pl.BlockSpec((tm, 128), lambda i: (i, 0)),
                  pl.BlockSpec((1, 1, d), lambda i: (i // nt, 0, 0)), _const_spec((1, d)),
                  pl.BlockSpec(memory_space=pl.ANY), pl.BlockSpec(memory_space=pl.ANY)],
        out_specs=row,
        out_shape=jax.ShapeDtypeStruct((t_tok, d), F32),
        scratch_shapes=[pltpu.SMEM((3, TOP_K, tm), jnp.int32)]
                       + [pltpu.VMEM((TOP_K, tm * 8, 128), jnp.uint32) for _ in range(3)]
                       + [pltpu.SemaphoreType.DMA((3,)), pltpu.SemaphoreType.DMA((3,))],
        compiler_params=_cparams(("arbitrary",)),
        name="final",
    )(x2.reshape(t_tok, d), w128, g_f, g_final, pos_t, ys)
    return out.reshape(b, s_len, d)


def _rope_tables(s_len, c_len):
    rows = s_len // GRID_W
    r = jnp.repeat(jnp.arange(rows, dtype=F32), GRID_W)
    cidx = jnp.tile(jnp.arange(GRID_W, dtype=F32), rows)
    n_freq = 16
    inv = ROPE_BASE ** (-jnp.arange(n_freq, dtype=F32) / n_freq)
    ang = jnp.concatenate([r[:, None] * inv, cidx[:, None] * inv], axis=-1)
    cos = jnp.concatenate([jnp.cos(ang), jnp.ones((c_len, 32), F32)], axis=0)
    sin = jnp.concatenate([jnp.sin(ang), jnp.zeros((c_len, 32), F32)], axis=0)
    rope_a = jnp.concatenate([jnp.tile(cos, (1, 4)), jnp.tile(sin, (1, 4))], axis=1)
    rope_k = jnp.concatenate([cos, cos, sin, sin], axis=1)
    rope_t = jnp.concatenate([cos.T, sin.T], axis=0)
    return rope_a, rope_k, rope_t


def _in_weight(w_in):
    qa_cols = np.empty((4, 2, 4, 32), np.int32)
    for r in range(4):
        for p in range(2):
            for j in range(4):
                qa_cols[r, p, j] = (4 * j + r) * 64 + p * 32 + np.arange(32)
    ka_cols = np.empty((2, 4, 32), np.int32)
    for p in range(2):
        for j in range(4):
            ka_cols[p, j] = 1024 + j * 64 + p * 32 + np.arange(32)
    cols = np.concatenate([qa_cols.reshape(-1), ka_cols.reshape(-1), np.arange(1280, 2368)])
    w = w_in[:, cols]
    kr = w_in[:, 2304:2368]
    kr_rot = jnp.concatenate([-kr[:, 32:], kr[:, :32]], axis=1)
    return jnp.concatenate([w, kr_rot], axis=1).astype(BF16)


def kernel(x, c, ctx, c_ctx, w_mod, b_mod, norm_attn_g, norm_ffn_g, w_in, attn_sink, q_a_norm_g, w_uq,
           kv_a_norm_g, w_ukv, w_out, w_router, router_bias, w_gate, w_up, w_down, ws_gate, ws_up, ws_down,
           norm_final_g):
    b, s_len, d = x.shape
    c_len = ctx.shape[1]
    l_total = s_len + c_len
    t_tok = b * s_len

    c_rows = jnp.zeros((8, d), F32).at[:b].set(c).at[b].set(c_ctx)
    mod = _mod(c_rows, w_mod[0], b_mod[0][None, :])
    mod6 = mod.reshape(8, 6, d)
    sh_a, sc_a, g_a, sh_f, sc_f, g_f = (mod6[:b, k] for k in range(6))
    ms_x = jnp.stack([sh_a, sc_a], axis=1)
    ms_c = jnp.broadcast_to(jnp.stack([mod6[b, 0], mod6[b, 1]], axis=0)[None], (b, 2, d))

    w_in_p = _in_weight(w_in[0])
    w_uq_h = w_uq[0].reshape(Q_LORA_RANK, B_HEADS, B_QK_DIM)
    w_uqt = jnp.pad(jnp.transpose(w_uq_h, (1, 2, 0)), ((0, 0), (0, HEAD_PAD - B_QK_DIM), (0, 0)))
    w_uqt = w_uqt.reshape(B_HEADS * HEAD_PAD, Q_LORA_RANK).astype(BF16)
    w_ukv_h = w_ukv[0].reshape(KV_LORA_RANK, B_HEADS, B_NOPE_DIM + B_V_DIM)
    w_uk = w_ukv_h[:, :, :B_NOPE_DIM].reshape(KV_LORA_RANK, B_HEADS * B_NOPE_DIM).astype(BF16)
    w_uvt = w_ukv_h[:, :, B_NOPE_DIM:].reshape(KV_LORA_RANK, B_HEADS * B_V_DIM).T.astype(BF16)
    rope_a, rope_k, rope_t = _rope_tables(s_len, c_len)

    g_attn = norm_attn_g[0][None, :]
    gq = q_a_norm_g[0][None, :]
    gkv = kv_a_norm_g[0][None, :]
    proj_w = (g_attn, w_in_p, gq, w_uqt, gkv, w_uk, w_uvt, rope_a, rope_k, rope_t)
    outs = _inproj(x, ms_x, *proj_w, l_total, 0)
    qa, ka, va, qbt, kb, vbt = _inproj(ctx, ms_c, *proj_w, l_total, s_len // ROW_TILE, prev=outs)

    stack_heads = np.array([4 * (s % 4) + s // 4 for s in range(16)], np.int32)
    sink_row = jnp.repeat(attn_sink[0][stack_heads] * LOG2E, SWA_BLOCK)[None, :].astype(F32)
    o_a = _swa(qa, ka, va, sink_row, s_len, c_len)
    o_b = _mla(qbt, kb, vbt, s_len)

    oa_rows = np.empty((4, 4, 64), np.int32)
    for r in range(4):
        for j in range(4):
            oa_rows[r, j] = (4 * j + r) * 64 + np.arange(64)
    woa = w_out[0][oa_rows.reshape(-1)].astype(BF16)
    wob = w_out[0][1024:].astype(BF16)
    wr = jnp.pad(w_router[0], ((0, 0), (0, 128 - N_EXPERTS)))
    wr_hi = wr.astype(BF16)
    wr = jnp.stack([wr_hi, (wr - wr_hi.astype(F32)).astype(BF16)], axis=0)
    mod4 = jnp.stack([g_a, sh_f, sc_f, g_f], axis=1)
    h2p, x2, idx_t, rank_t, w128, cnt = _post(
        o_a, o_b, x, mod4, norm_ffn_g[0][None, :], woa, wob, wr, router_bias[0][:, None].astype(F32),
        ws_gate[0].astype(BF16), ws_up[0].astype(BF16), ws_down[0].astype(BF16))

    tm = EXPERT_TILE
    counts = cnt[:, 0].astype(jnp.int32)
    padded = (counts + tm - 1) // tm * tm
    pad_end = jnp.cumsum(padded).astype(jnp.int32)
    pad_start = pad_end - padded
    n_tiles = -(-t_tok * TOP_K // tm) + N_EXPERTS
    tile_row0 = jnp.arange(n_tiles, dtype=jnp.int32) * tm
    tile_expert = jnp.minimum(jnp.sum((pad_end[None, :] <= tile_row0[:, None]).astype(jnp.int32), axis=1),
                              N_EXPERTS - 1)
    n_used = (pad_end[-1] // tm).astype(jnp.int32)[None]

    pos_t = _plan(pad_start, idx_t, rank_t)
    xs = _dispatch(pad_start, pad_end, h2p, pos_t, n_tiles * tm)
    ys = _experts(tile_expert, n_used, xs, w_gate[0], w_up[0], w_down[0])
    return _final(x2, w128, g_f[:, None, :], norm_final_g[None, :], pos_t, ys)
```
